```python
import jax, jax.numpy as jnp
from jax import lax
import numpy as np

D_MODEL = 1024
BATCH = 2
SEQ = 16384
DEPTH = 2

HEAD_DIM = 64
RWKV_WIDTH = D_MODEL // 2
FOX_WIDTH = D_MODEL - RWKV_WIDTH
MIX_WIDTH = RWKV_WIDTH + FOX_WIDTH
RWKV_HEADS = RWKV_WIDTH // HEAD_DIM
FOX_HEADS = FOX_WIDTH // HEAD_DIM
DECAY_LORA = 64
ICLR_LORA = 64
Q_BLOCK = 128
LN_EPS = 1e-5
GN_EPS = 64e-5
DEEPNORM_ALPHA = (2 * DEPTH) ** 0.25
DEEPNORM_BETA = (8 * DEPTH) ** -0.25

RW_R0 = 0
RW_K0 = RW_R0 + RWKV_WIDTH
RW_V0 = RW_K0 + RWKV_WIDTH
RW_WD0 = RW_V0 + RWKV_WIDTH
RW_AD0 = RW_WD0 + DECAY_LORA
RW_END = RW_AD0 + ICLR_LORA
FX_Q0 = RW_END
FX_K0 = FX_Q0 + FOX_WIDTH
FX_V0 = FX_K0 + FOX_WIDTH
FX_F0 = FX_V0 + FOX_WIDTH
FX_END = FX_F0 + FOX_HEADS
GATE0 = FX_END
P_TOTAL = GATE0 + MIX_WIDTH

kernel_name = "hymba_rwkv7_fox_deepnorm_adaln"


def _layer_norm(x, g, b, eps=LN_EPS):
    x32 = x.astype(jnp.float32)
    mu = jnp.mean(x32, axis=-1, keepdims=True)
    var = jnp.mean(jnp.square(x32 - mu), axis=-1, keepdims=True)
    return ((x32 - mu) * lax.rsqrt(var + eps)).astype(x.dtype) * g + b


def _rwkv7_scan(r, decay, k, v, kk, kka):
    B, T, H, N = r.shape

    def step(S, inp):
        r_t, w_t, k_t, v_t, kk_t, b_t = inp
        sa = jnp.einsum('bhvk,bhk->bhv', S, kk_t)
        S = (S * w_t[:, :, None, :]
             - sa[..., None] * b_t[:, :, None, :]
             + v_t[..., None] * k_t[:, :, None, :])
        y = jnp.einsum('bhvk,bhk->bhv', S, r_t)
        return S, y

    S0 = jnp.zeros((B, H, N, N), jnp.float32)
    xs = (jnp.moveaxis(r, 1, 0), jnp.moveaxis(decay, 1, 0), jnp.moveaxis(k, 1, 0),
          jnp.moveaxis(v, 1, 0), jnp.moveaxis(kk, 1, 0), jnp.moveaxis(kka, 1, 0))
    _, ys = lax.scan(step, S0, xs)
    return jnp.moveaxis(ys, 0, 1)


def _fox_attention(q, k, v, log_f):
    B, T, H, D = q.shape
    qh = q.transpose(0, 2, 1, 3)
    kh = k.transpose(0, 2, 1, 3)
    vh = v.transpose(0, 2, 1, 3)
    cum = jnp.cumsum(log_f, axis=1).transpose(0, 2, 1)
    key_pos = jnp.arange(T)
    scale = D ** -0.5

    def block(i):
        start = i * Q_BLOCK
        qb = lax.dynamic_slice_in_dim(qh, start, Q_BLOCK, axis=2)
        cb = lax.dynamic_slice_in_dim(cum, start, Q_BLOCK, axis=2)
        s = (jnp.einsum('bhqd,bhkd->bhqk', qb, kh).astype(jnp.float32) * scale
             + cb[..., None] - cum[:, :, None, :])
        q_pos = start + jnp.arange(Q_BLOCK)
        s = jnp.where(q_pos[:, None] >= key_pos[None, :], s, -jnp.inf)
        p = jax.nn.softmax(s, axis=-1)
        return jnp.einsum('bhqk,bhkd->bhqd', p.astype(vh.dtype), vh)

    out = lax.map(block, jnp.arange(T // Q_BLOCK))
    return out.transpose(1, 0, 3, 2, 4).reshape(B, T, H * D)


def _hybrid_layer(x, c, w_ada, b_ada, w_in, rwkv_mix, w0, w_up, a0, a_up, k_k, k_a, r_k,
                  gn_g, gn_b, fox_bf, w_out, ln_g, ln_b):
    B, T, _ = x.shape
    mod = c @ w_ada + b_ada
    shift, scale, gate = jnp.split(mod, 3, axis=-1)
    h = x * (1.0 + scale[:, None, :]) + shift[:, None, :]

    proj = h @ w_in

    rw = proj[..., :RW_END]
    rw_prev = jnp.pad(rw, ((0, 0), (1, 0), (0, 0)))[:, :-1]
    rw = rw + (rw_prev - rw) * rwkv_mix
    hs = (B, T, RWKV_HEADS, HEAD_DIM)
    r = rw[..., RW_R0:RW_K0].astype(jnp.float32)
    k = rw[..., RW_K0:RW_V0].astype(jnp.float32)
    v = rw[..., RW_V0:RW_WD0].astype(jnp.float32)
    w_low = rw[..., RW_WD0:RW_AD0].astype(jnp.float32)
    a_low = rw[..., RW_AD0:RW_END].astype(jnp.float32)
    w_log = -jax.nn.softplus(-(w0 + jnp.tanh(w_low) @ w_up)) - 0.5
    decay = jnp.exp(-jnp.exp(w_log))
    a = jax.nn.sigmoid(a0 + a_low @ a_up)
    kk = (k * k_k).reshape(hs)
    kk = kk / jnp.maximum(jnp.sqrt(jnp.sum(kk * kk, axis=-1, keepdims=True)), 1e-12)
    k = k * (1.0 + (a - 1.0) * k_a)
    r_h, k_h, v_h, a_h = r.reshape(hs), k.reshape(hs), v.reshape(hs), a.reshape(hs)
    y_a = _rwkv7_scan(r_h, decay.reshape(hs), k_h, v_h, kk, kk * a_h)
    mu = jnp.mean(y_a, axis=-1, keepdims=True)
    var = jnp.mean(jnp.square(y_a - mu), axis=-1, keepdims=True)
    y_a = ((y_a - mu) * lax.rsqrt(var + GN_EPS)).reshape(B, T, RWKV_WIDTH) * gn_g + gn_b
    bonus = jnp.sum(r_h * k_h * r_k.reshape(RWKV_HEADS, HEAD_DIM), axis=-1, keepdims=True) * v_h
    y_a = (y_a + bonus.reshape(B, T, RWKV_WIDTH)).astype(x.dtype)

    fs = (B, T, FOX_HEADS, HEAD_DIM)
    q_f = proj[..., FX_Q0:FX_K0].reshape(fs)
    k_f = proj[..., FX_K0:FX_V0].reshape(fs)
    v_f = proj[..., FX_V0:FX_F0].reshape(fs)
    log_f = jax.nn.log_sigmoid(proj[..., FX_F0:FX_END].astype(jnp.float32) + fox_bf)
    y_b = _fox_attention(q_f, k_f, v_f, log_f).astype(x.dtype)

    g_path = jax.nn.silu(proj[..., GATE0:P_TOTAL])
    y = jnp.concatenate([y_a, y_b], axis=-1) * g_path
    out = y @ w_out

    return _layer_norm(DEEPNORM_ALPHA * x + (1.0 + gate[:, None, :]) * out, ln_g, ln_b)


def setup_inputs(seed: int = 0) -> dict:
    key = jax.random.key(seed)
    ks = jax.random.split(key, 24)
    f32 = jnp.float32
    nrm = lambda k, shape, s: (jax.random.normal(k, shape, f32) * s)

    x = jax.random.normal(ks[0], (BATCH, SEQ, D_MODEL), f32)
    c = jax.random.normal(ks[1], (BATCH, D_MODEL), f32)
    emb_ln_g = 1.0 + nrm(ks[2], (D_MODEL,), 0.02)
    emb_ln_b = nrm(ks[3], (D_MODEL,), 0.02)

    w_ada = nrm(ks[4], (DEPTH, D_MODEL, 3 * D_MODEL), 0.1 * D_MODEL ** -0.5)
    b_ada = nrm(ks[5], (DEPTH, 3 * D_MODEL), 0.01)
    col_scale = (jnp.ones((P_TOTAL,), f32)
                 .at[RW_V0:RW_WD0].set(DEEPNORM_BETA)
                 .at[FX_V0:FX_F0].set(DEEPNORM_BETA))
    w_in = nrm(ks[6], (DEPTH, D_MODEL, P_TOTAL), D_MODEL ** -0.5) * col_scale
    rwkv_mix = jax.random.uniform(ks[7], (DEPTH, RW_END), f32)
    w0 = jax.random.uniform(ks[8], (DEPTH, RWKV_WIDTH), f32, -6.0, -1.0)
    w_up = nrm(ks[9], (DEPTH, DECAY_LORA, RWKV_WIDTH), 0.1 * DECAY_LORA ** -0.5)
    a0 = nrm(ks[10], (DEPTH, RWKV_WIDTH), 0.1)
    a_up = nrm(ks[11], (DEPTH, ICLR_LORA, RWKV_WIDTH), 0.1 * ICLR_LORA ** -0.5)
    k_k = 0.85 + nrm(ks[12], (DEPTH, RWKV_WIDTH), 0.02)
    k_a = 1.0 + nrm(ks[13], (DEPTH, RWKV_WIDTH), 0.02)
    r_k = nrm(ks[14], (DEPTH, RWKV_WIDTH), 0.1)
    gn_g = 1.0 + nrm(ks[15], (DEPTH, RWKV_WIDTH), 0.02)
    gn_b = nrm(ks[16], (DEPTH, RWKV_WIDTH), 0.02)
    fox_bf = 3.0 + nrm(ks[17], (DEPTH, FOX_HEADS), 0.5)
    w_out = nrm(ks[18], (DEPTH, MIX_WIDTH, D_MODEL), MIX_WIDTH ** -0.5) * DEEPNORM_BETA
    ln_g = 1.0 + nrm(ks[19], (DEPTH, D_MODEL), 0.02)
    ln_b = nrm(ks[20], (DEPTH, D_MODEL), 0.02)
    return {"x": x, "c": c, "emb_ln_g": emb_ln_g, "emb_ln_b": emb_ln_b,
            "w_ada": w_ada, "b_ada": b_ada, "w_in": w_in, "rwkv_mix": rwkv_mix,
            "w0": w0, "w_up": w_up, "a0": a0, "a_up": a_up, "k_k": k_k, "k_a": k_a,
            "r_k": r_k, "gn_g": gn_g, "gn_b": gn_b, "fox_bf": fox_bf, "w_out": w_out,
            "ln_g": ln_g, "ln_b": ln_b}


def reference(x, c, emb_ln_g, emb_ln_b, w_ada, b_ada, w_in, rwkv_mix, w0, w_up, a0, a_up,
              k_k, k_a, r_k, gn_g, gn_b, fox_bf, w_out, ln_g, ln_b):
    h = _layer_norm(x, emb_ln_g, emb_ln_b)
    for l in range(DEPTH):
        h = _hybrid_layer(h, c, w_ada[l], b_ada[l], w_in[l], rwkv_mix[l], w0[l], w_up[l],
                          a0[l], a_up[l], k_k[l], k_a[l], r_k[l], gn_g[l], gn_b[l],
                          fox_bf[l], w_out[l], ln_g[l], ln_b[l])
    return h
```

```python
import functools

import jax
import jax.numpy as jnp
from jax import lax
from jax.experimental import pallas as pl
from jax.experimental.pallas import tpu as pltpu

F32 = jnp.float32
BF16 = jnp.bfloat16

HEAD_DIM = 64
LANES = 128
PAIR = LANES // HEAD_DIM
LORA = 64
LN_EPS = 1e-5
GN_EPS = 64e-5
CHUNK = 64
VMEM_LIMIT = 56 * 1024 * 1024


def _params(sem):
    return pltpu.CompilerParams(dimension_semantics=sem, vmem_limit_bytes=VMEM_LIMIT)


def _dot(a, b):
    return jnp.dot(a.astype(BF16), b.astype(BF16), preferred_element_type=F32)


def _dot_nt(a, b):
    return lax.dot_general(a.astype(BF16), b.astype(BF16), (((1,), (1,)), ((), ())),
                           preferred_element_type=F32)


def _split3(x):
    hi = x.astype(BF16)
    r1 = x - hi.astype(F32)
    mid = r1.astype(BF16)
    lo = (r1 - mid.astype(F32)).astype(BF16)
    return hi, mid, lo


def _dot_exact_rhs(a_bf16, x):
    hi, mid, lo = _split3(x)
    f = lambda y: jnp.dot(a_bf16, y, preferred_element_type=F32)
    return f(hi) + f(mid) + f(lo)


def _dot_exact_lhs(x, a_bf16):
    hi, mid, lo = _split3(x)
    f = lambda y: jnp.dot(y, a_bf16, preferred_element_type=F32)
    return f(hi) + f(mid) + f(lo)


def _softplus(z):
    return jnp.maximum(z, 0.0) + jnp.log(1.0 + jnp.exp(-jnp.abs(z)))


def _layer_norm_rows(x, g, b):
    mu = jnp.mean(x, axis=-1, keepdims=True)
    d = x - mu
    var = jnp.mean(d * d, axis=-1, keepdims=True)
    return d * lax.rsqrt(var + LN_EPS) * g + b


def _ada_kernel(c_ref, w_ref, b_ref, o_ref):
    c = c_ref[...]
    w = w_ref[0]
    ch, cm, cl = _split3(c)
    wh, wm, wl = _split3(w)
    f = lambda a, b: jnp.dot(a, b, preferred_element_type=F32)
    acc = f(ch, wh) + (f(ch, wm) + f(cm, wh)) + (f(ch, wl) + f(cm, wm) + f(cl, wh))
    o_ref[0] = acc + b_ref[0]


def _ada_call(c8, w_ada, b_ada):
    depth, d, d3 = w_ada.shape
    return pl.pallas_call(
        _ada_kernel,
        name="ada",
        out_shape=jax.ShapeDtypeStruct((depth, 8, d3), F32),
        grid=(depth,),
        in_specs=[pl.BlockSpec((8, d), lambda l: (0, 0)),
                  pl.BlockSpec((1, d, d3), lambda l: (l, 0, 0)),
                  pl.BlockSpec((1, 1, d3), lambda l: (l, 0, 0))],
        out_specs=pl.BlockSpec((1, 8, d3), lambda l: (l, 0, 0)),
        compiler_params=_params(("arbitrary",)),
    )(c8, w_ada, b_ada.reshape(depth, 1, d3))


def _ln_kernel(x_ref, g_ref, b_ref, o_ref):
    o_ref[0] = _layer_norm_rows(x_ref[0], g_ref[...], b_ref[...])


def _ln_call(x, g, b, tm):
    bsz, t, d = x.shape
    return pl.pallas_call(
        _ln_kernel,
        name="emb_ln",
        out_shape=jax.ShapeDtypeStruct(x.shape, F32),
        grid=(bsz, t // tm),
        in_specs=[pl.BlockSpec((1, tm, d), lambda i, j: (i, j, 0)),
                  pl.BlockSpec((1, d), lambda i, j: (0, 0)),
                  pl.BlockSpec((1, d), lambda i, j: (0, 0))],
        out_specs=pl.BlockSpec((1, tm, d), lambda i, j: (i, j, 0)),
        compiler_params=_params(("arbitrary", "arbitrary")),
    )(x, g.reshape(1, d), b.reshape(1, d))


def _proj_kernel(x_ref, sc_ref, sh_ref, wrw_ref, wqkv_ref, wg_ref, wf_ref, bf_ref, ex_ref,
                 rw_ref, qkv_ref, g_ref, cq_ref, cl_ref, carry_ref, *, fox_width):
    tm = x_ref.shape[1]

    @pl.when(pl.program_id(1) == 0)
    def _():
        carry_ref[...] = jnp.zeros_like(carry_ref)

    h = x_ref[0] * (1.0 + sc_ref[0]) + sh_ref[0]
    hb = h.astype(BF16)
    rw_ref[0] = jnp.dot(hb, wrw_ref[...], preferred_element_type=F32)
    qkv = jnp.dot(hb, wqkv_ref[...], preferred_element_type=F32)
    col = lax.broadcasted_iota(jnp.int32, qkv.shape, 1)
    qkv = jnp.where(col < fox_width, qkv * (HEAD_DIM ** -0.5), qkv)
    qkv_ref[0] = qkv.astype(BF16)
    g_ref[0] = jnp.dot(hb, wg_ref[...], preferred_element_type=F32)

    z = jnp.dot(hb, wf_ref[...], preferred_element_type=F32) + bf_ref[...]
    lf = -_softplus(-z)
    r = lax.broadcasted_iota(jnp.int32, (tm, tm), 0)
    c = lax.broadcasted_iota(jnp.int32, (tm, tm), 1)
    tri = jnp.where(r >= c, 1.0, 0.0).astype(BF16)
    cum = _dot_exact_rhs(tri, lf) + carry_ref[...]
    carry_ref[...] = cum[tm - 1:tm, :]
    cq_ref[0] = _dot_exact_lhs(cum, ex_ref[...])
    sel_r = lax.broadcasted_iota(jnp.int32, (8, LANES), 0)
    sel_c = lax.broadcasted_iota(jnp.int32, (8, LANES), 1)
    sel = jnp.where(sel_r == sel_c, 1.0, 0.0).astype(BF16)
    hi, mid, lo = _split3(cum)
    f = lambda y: lax.dot_general(sel, y, (((1,), (1,)), ((), ())), preferred_element_type=F32)
    cl_ref[0] = f(hi) + f(mid) + f(lo)


def _proj_call(x, scale, shift, w_rw, w_qkv, w_g, w_f, bf, expand, tm):
    bsz, t, d = x.shape
    n_rw, n_qkv, n_g = w_rw.shape[1], w_qkv.shape[1], w_g.shape[1]
    fox_width = n_qkv // 3
    full = lambda a: pl.BlockSpec(a.shape, lambda i, j: (0,) * a.ndim)
    rows = lambda n: pl.BlockSpec((1, tm, n), lambda i, j: (i, j, 0))
    vec = pl.BlockSpec((1, 1, d), lambda i, j: (i, 0, 0))
    return pl.pallas_call(
        functools.partial(_proj_kernel, fox_width=fox_width),
        name="proj",
        out_shape=(jax.ShapeDtypeStruct((bsz, t, n_rw), F32),
                   jax.ShapeDtypeStruct((bsz, t, n_qkv), BF16),
                   jax.ShapeDtypeStruct((bsz, t, n_g), F32),
                   jax.ShapeDtypeStruct((bsz, t, fox_width), F32),
                   jax.ShapeDtypeStruct((bsz, 8, t), F32)),
        grid=(bsz, t // tm),
        in_specs=[rows(d), vec, vec, full(w_rw), full(w_qkv), full(w_g), full(w_f), full(bf),
                  full(expand)],
        out_specs=(rows(n_rw), rows(n_qkv), rows(n_g), rows(fox_width),
                   pl.BlockSpec((1, 8, tm), lambda i, j: (i, 0, j))),
        scratch_shapes=[pltpu.VMEM((1, LANES), F32)],
        compiler_params=_params(("arbitrary", "arbitrary")),
    )(x, scale, shift, w_rw, w_qkv, w_g, w_f, bf, expand)


_P_MIX_R, _P_MIX_K, _P_MIX_V, _P_W0, _P_A0, _P_KK, _P_KA, _P_RK, _P_GNG, _P_GNB = range(10)
_P_ROWS = 16


def _rwkv_kernel(r_ref, k_ref, v_ref, wa_ref, pp_ref, mixwa_ref, wcomb_ref, o_ref,
                 carry_ref, state_ref):
    ct = r_ref.shape[1]
    n_chunks = ct // CHUNK
    C = CHUNK

    @pl.when(pl.program_id(2) == 0)
    def _():
        carry_ref[...] = jnp.zeros_like(carry_ref)
        state_ref[...] = jnp.zeros_like(state_ref)

    pp = pp_ref[0]
    prow = lambda i: pp[i:i + 1, :]
    row_id = lax.broadcasted_iota(jnp.int32, (ct, LANES), 0)
    lane_id = lax.broadcasted_iota(jnp.int32, (ct, LANES), 1)

    def shifted(ref, slot, mix):
        x = ref[0]
        prev = pltpu.roll(x, 1, 0)
        prev = jnp.where(row_id == 0, carry_ref[slot:slot + 1, :], prev)
        carry_ref[slot:slot + 1, :] = x[ct - 1:ct, :]
        return x + (prev - x) * mix

    r = shifted(r_ref, 0, prow(_P_MIX_R))
    k = shifted(k_ref, 1, prow(_P_MIX_K))
    v = shifted(v_ref, 2, prow(_P_MIX_V))
    wa = shifted(wa_ref, 3, mixwa_ref[...])

    t_in = jnp.where(lane_id < LORA, jnp.tanh(wa), wa)
    dwa = _dot(t_in, wcomb_ref[0])
    w_log = -_softplus(-(prow(_P_W0) + dwa[:, :LANES])) - 0.5
    lw = -jnp.exp(w_log)
    a = 1.0 / (1.0 + jnp.exp(-(prow(_P_A0) + dwa[:, LANES:])))

    bd_r = lax.broadcasted_iota(jnp.int32, (LANES, LANES), 0)
    bd_c = lax.broadcasted_iota(jnp.int32, (LANES, LANES), 1)
    same_head = (bd_r // HEAD_DIM) == (bd_c // HEAD_DIM)
    ones_bd = jnp.where(same_head, 1.0, 0.0).astype(BF16)

    kk = k * prow(_P_KK)
    ss = _dot_exact_lhs(kk * kk, ones_bd)
    kap = kk / jnp.maximum(jnp.sqrt(ss), 1e-12)
    k2 = k * (1.0 + (a - 1.0) * prow(_P_KA))
    b = kap * a
    bonus = _dot_exact_lhs(r * k2 * prow(_P_RK), ones_bd) * v

    lane_c = lax.broadcasted_iota(jnp.int32, (C, LANES), 1)
    m1 = jnp.where(lane_c < HEAD_DIM, 1.0, 0.0)
    m2 = 1.0 - m1
    stack = lambda x: jnp.concatenate([x * m1, x * m2], axis=0)
    dup = lambda x: jnp.concatenate([x, x], axis=0)
    sm = jnp.concatenate([m1, m2], axis=0)
    strict_lower = same_head & ((bd_r % C) > (bd_c % C))
    lower = same_head & ((bd_r % C) >= (bd_c % C))
    eye = bd_r == bd_c
    tr = lax.broadcasted_iota(jnp.int32, (C, C), 0)
    tc = lax.broadcasted_iota(jnp.int32, (C, C), 1)
    tri = jnp.where(tr >= tc, 1.0, 0.0).astype(BF16)
    gn_g, gn_b = prow(_P_GNG), prow(_P_GNB)

    state = state_ref[...]
    for ci in range(n_chunks):
        sl = slice(ci * C, (ci + 1) * C)
        lw_c = lw[sl]
        L = _dot_exact_rhs(tri, lw_c)
        Lc = L[C - 1:C, :]
        e_pos = jnp.exp(L)
        e_neg = jnp.exp(-L)
        e_end = jnp.exp(Lc - L)
        rt = r[sl] * e_pos
        kt = kap[sl] * jnp.exp(L - lw_c)
        kh = k2[sl] * e_neg
        bh = b[sl] * e_neg
        Ks, Rs, Vs = stack(kt), stack(rt), stack(v[sl])
        K2, B2 = dup(kh), dup(bh)
        N = jnp.where(strict_lower, _dot_nt(Ks, B2), 0.0)
        Akk = jnp.where(strict_lower, _dot_nt(Ks, K2), 0.0)
        Ark = jnp.where(lower, _dot_nt(Rs, K2), 0.0)
        Arb = jnp.where(lower, _dot_nt(Rs, B2), 0.0)
        Tm = jnp.where(eye, 1.0, 0.0) - N
        P = N
        for _ in range(C.bit_length() - 2):
            P = _dot(P, P)
            Tm = Tm + _dot(Tm, P)
        P1 = _dot(Tm, Ks)
        P2 = _dot(Tm, _dot(Akk, Vs))
        Q1 = Rs - _dot(Arb, P1)
        Q2 = _dot(Ark, Vs) - _dot(Arb, P2)
        KendT = stack(k2[sl] * e_end).T
        BendT = stack(b[sl] * e_end).T
        G = jnp.where(eye, jnp.broadcast_to(jnp.exp(Lc), (LANES, LANES)), 0.0) - _dot(BendT, P1)
        H = _dot(KendT, Vs) - _dot(BendT, P2)
        ys = _dot(Q1, state) + Q2
        state = _dot(G, state) + H
        mu = jnp.sum(ys, axis=-1, keepdims=True) * (1.0 / HEAD_DIM)
        d = (ys - mu) * sm
        var = jnp.sum(d * d, axis=-1, keepdims=True) * (1.0 / HEAD_DIM)
        yn = d * lax.rsqrt(var + GN_EPS)
        o_ref[0, sl, :] = (yn[:C] + yn[C:]) * gn_g + gn_b + bonus[sl]
    state_ref[...] = state


def _rwkv_call(rw, pp, mixwa, wcomb, width, ct):
    bsz, t, _ = rw.shape
    n_pairs = width // LANES
    blk = lambda off: pl.BlockSpec((1, ct, LANES), lambda i, p, j, off=off: (i, j, off + p))
    wa_blk = pl.BlockSpec((1, ct, LANES), lambda i, p, j: (i, j, 3 * n_pairs))
    return pl.pallas_call(
        _rwkv_kernel,
        name="rwkv",
        out_shape=jax.ShapeDtypeStruct((bsz, t, width), F32),
        grid=(bsz, n_pairs, t // ct),
        in_specs=[blk(0), blk(n_pairs), blk(2 * n_pairs), wa_blk,
                  pl.BlockSpec((1, _P_ROWS, LANES), lambda i, p, j: (p, 0, 0)),
                  pl.BlockSpec((1, LANES), lambda i, p, j: (0, 0)),
                  pl.BlockSpec((1, LANES, 2 * LANES), lambda i, p, j: (p, 0, 0))],
        out_specs=pl.BlockSpec((1, ct, LANES), lambda i, p, j: (i, j, p)),
        scratch_shapes=[pltpu.VMEM((8, LANES), F32), pltpu.VMEM((LANES, LANES), F32)],
        compiler_params=_params(("arbitrary", "arbitrary", "arbitrary")),
    )(rw, rw, rw, rw, pp, mixwa, wcomb)


def _fox_kernel(q_ref, k_ref, v_ref, cq_ref, cl_ref, o_ref, m_ref, l_ref, acc_ref, *, bk):
    bq = q_ref.shape[1]
    qi = pl.program_id(2)
    lane = lax.broadcasted_iota(jnp.int32, (bq, LANES), 1)
    first = lane < HEAD_DIM
    q = q_ref[0]
    zero = jnp.zeros_like(q)
    qh = (jnp.where(first, q, zero), jnp.where(first, zero, q))
    cq = cq_ref[0]
    cqh = (cq[:, 0:1], cq[:, HEAD_DIM:HEAD_DIM + 1])

    m_ref[...] = jnp.full_like(m_ref, -jnp.inf)
    l_ref[...] = jnp.zeros_like(l_ref)
    acc_ref[...] = jnp.zeros_like(acc_ref)

    def block(j, masked):
        start = pl.multiple_of(j * bk, bk)
        kb = k_ref[0, pl.ds(start, bk), :]
        vb = v_ref[0, pl.ds(start, bk), :]
        for h in range(PAIR):
            ck = cl_ref[0, 0, h:h + 1, pl.ds(start, bk)]
            s = lax.dot_general(qh[h], kb, (((1,), (1,)), ((), ())), preferred_element_type=F32)
            s = s + (cqh[h] - ck)
            if masked:
                rr = lax.broadcasted_iota(jnp.int32, (bq, bk), 0)
                cc = lax.broadcasted_iota(jnp.int32, (bq, bk), 1)
                s = jnp.where(rr >= cc, s, -jnp.inf)
            m_old = m_ref[h]
            m_new = jnp.maximum(m_old, jnp.max(s, axis=-1, keepdims=True))
            alpha = jnp.exp(m_old - m_new)
            p = jnp.exp(s - m_new)
            l_ref[h] = alpha * l_ref[h] + jnp.sum(p, axis=-1, keepdims=True)
            acc_ref[h] = alpha * acc_ref[h] + jnp.dot(p.astype(BF16), vb, preferred_element_type=F32)
            m_ref[h] = m_new

    def body(j, carry):
        block(j, False)
        return carry

    lax.fori_loop(0, qi * (bq // bk), body, 0)
    for d in range(bq // bk):
        block(qi * (bq // bk) + d, True)
    out = jnp.where(first, acc_ref[0] / l_ref[0], acc_ref[1] / l_ref[1])
    o_ref[0] = out


def _fox_call(qkv, cq, cl, width, bq):
    bsz, t, _ = qkv.shape
    n_pairs = width // LANES
    bk = bq
    cl4 = cl.reshape(bsz, n_pairs, PAIR, t)
    return pl.pallas_call(
        functools.partial(_fox_kernel, bk=bk),
        name="fox",
        out_shape=jax.ShapeDtypeStruct((bsz, t, width), F32),
        grid=(bsz, n_pairs, t // bq),
        in_specs=[pl.BlockSpec((1, bq, LANES), lambda i, p, j: (i, j, p)),
                  pl.BlockSpec((1, t, LANES), lambda i, p, j: (i, 0, n_pairs + p)),
                  pl.BlockSpec((1, t, LANES), lambda i, p, j: (i, 0, 2 * n_pairs + p)),
                  pl.BlockSpec((1, bq, LANES), lambda i, p, j: (i, j, p)),
                  pl.BlockSpec((1, 1, PAIR, t), lambda i, p, j: (i, p, 0, 0))],
        out_specs=pl.BlockSpec((1, bq, LANES), lambda i, p, j: (i, j, p)),
        scratch_shapes=[pltpu.VMEM((PAIR, bq, 1), F32), pltpu.VMEM((PAIR, bq, 1), F32),
                        pltpu.VMEM((PAIR, bq, LANES), F32)],
        compiler_params=_params(("arbitrary", "arbitrary", "arbitrary")),
    )(qkv, qkv, qkv, cq, cl4)


def _merge_kernel(ya_ref, yb_ref, g_ref, x_ref, gc_ref, wa_ref, wb_ref, lg_ref, lb_ref, o_ref,
                  *, alpha):
    half = ya_ref.shape[2]
    g = g_ref[0]
    silu = g / (1.0 + jnp.exp(-g))
    out = _dot(ya_ref[0] * silu[:, :half], wa_ref[...]) + _dot(yb_ref[0] * silu[:, half:], wb_ref[...])
    z = alpha * x_ref[0] + (1.0 + gc_ref[0]) * out
    o_ref[0] = _layer_norm_rows(z, lg_ref[...], lb_ref[...])


def _merge_call(ya, yb, g, x, gate_c, w_a, w_b, ln_g, ln_b, alpha, tm):
    bsz, t, d = x.shape
    half = ya.shape[2]
    rows = lambda n: pl.BlockSpec((1, tm, n), lambda i, j: (i, j, 0))
    full = lambda a: pl.BlockSpec(a.shape, lambda i, j: (0,) * a.ndim)
    return pl.pallas_call(
        functools.partial(_merge_kernel, alpha=alpha),
        name="merge",
        out_shape=jax.ShapeDtypeStruct(x.shape, F32),
        grid=(bsz, t // tm),
        in_specs=[rows(half), rows(half), rows(g.shape[2]), rows(d),
                  pl.BlockSpec((1, 1, d), lambda i, j: (i, 0, 0)),
                  full(w_a), full(w_b), full(ln_g), full(ln_b)],
        out_specs=rows(d),
        compiler_params=_params(("arbitrary", "arbitrary")),
    )(ya, yb, g, x, gate_c, w_a, w_b, ln_g, ln_b)


def _layer_tables(rwkv_mix, w0, w_up, a0, a_up, k_k, k_a, r_k, gn_g, gn_b, width):
    n_pairs = width // LANES
    per_pair = lambda vct: vct.reshape(n_pairs, 1, LANES)
    rows = [rwkv_mix[0:width], rwkv_mix[width:2 * width], rwkv_mix[2 * width:3 * width],
            w0, a0, k_k, k_a, r_k, gn_g, gn_b]
    pp = jnp.concatenate([per_pair(x) for x in rows]
                         + [jnp.zeros((n_pairs, _P_ROWS - len(rows), LANES), F32)], axis=1)
    mixwa = rwkv_mix[3 * width:].reshape(1, 2 * LORA)
    zeros = jnp.zeros((n_pairs, LORA, LANES), F32)
    wu = w_up.reshape(LORA, n_pairs, LANES).transpose(1, 0, 2)
    au = a_up.reshape(LORA, n_pairs, LANES).transpose(1, 0, 2)
    wcomb = jnp.concatenate([jnp.concatenate([wu, zeros], axis=2),
                             jnp.concatenate([zeros, au], axis=2)], axis=1).astype(BF16)
    return pp, mixwa, wcomb


def kernel(x, c, emb_ln_g, emb_ln_b, w_ada, b_ada, w_in, rwkv_mix, w0, w_up, a0, a_up, k_k, k_a,
           r_k, gn_g, gn_b, fox_bf, w_out, ln_g, ln_b):
    bsz, t, d = x.shape
    depth = w_ada.shape[0]
    width = d // 2
    heads = width // HEAD_DIM
    rw_end = 3 * width + 2 * LORA
    fx_end = rw_end + 3 * width + heads
    alpha = (2 * depth) ** 0.25
    tm = min(512, t)
    ct = min(512, t)
    bq = min(512, t)

    c8 = jnp.pad(c, ((0, 8 - bsz), (0, 0)))
    mods = _ada_call(c8, w_ada, b_ada)[:, :bsz]
    expand = (jnp.arange(LANES)[:, None] == (jnp.arange(width)[None, :] // HEAD_DIM)).astype(BF16)

    h = _ln_call(x, emb_ln_g, emb_ln_b, tm)
    for l in range(depth):
        shift = mods[l, :, 0:d].reshape(bsz, 1, d)
        scale = mods[l, :, d:2 * d].reshape(bsz, 1, d)
        gate_c = mods[l, :, 2 * d:].reshape(bsz, 1, d)
        w = w_in[l]
        w_rw = w[:, :rw_end].astype(BF16)
        w_qkv = w[:, rw_end:rw_end + 3 * width].astype(BF16)
        w_f = jnp.pad(w[:, rw_end + 3 * width:fx_end], ((0, 0), (0, LANES - heads))).astype(BF16)
        w_g = w[:, fx_end:].astype(BF16)
        bf = jnp.pad(fox_bf[l], (0, LANES - heads)).reshape(1, LANES)
        rw, qkv, g, cq, cl = _proj_call(h, scale, shift, w_rw, w_qkv, w_g, w_f, bf, expand, tm)

        pp, mixwa, wcomb = _layer_tables(rwkv_mix[l], w0[l], w_up[l], a0[l], a_up[l], k_k[l],
                                         k_a[l], r_k[l], gn_g[l], gn_b[l], width)
        ya = _rwkv_call(rw, pp, mixwa, wcomb, width, ct)
        yb = _fox_call(qkv, cq, cl, width, bq)
        wo = w_out[l].astype(BF16)
        h = _merge_call(ya, yb, g, h, gate_c, wo[:width], wo[width:], ln_g[l].reshape(1, d),
                        ln_b[l].reshape(1, d), alpha, tm)
    return h
```

```python
import functools

import jax
import jax.numpy as jnp
from jax import lax
from jax.experimental import pallas as pl
from jax.experimental.pallas import tpu as pltpu

F32 = jnp.float32
BF16 = jnp.bfloat16

HEAD_DIM = 64
LANES = 128
SUBLANES = 8
PAIR = LANES // HEAD_DIM
LORA = 64
LN_EPS = 1e-5
GN_EPS = 64e-5
CHUNK = 64
TILE = 512
N_SPLIT = 3
LOG2E = 1.4426950408889634
VMEM_LIMIT = 56 * 1024 * 1024


def _params(sem):
    return pltpu.CompilerParams(dimension_semantics=sem, vmem_limit_bytes=VMEM_LIMIT)


def _dot(a, b):
    return jnp.dot(a.astype(BF16), b.astype(BF16), preferred_element_type=F32)


def _dot_nt(a, b):
    return lax.dot_general(a.astype(BF16), b.astype(BF16), (((1,), (1,)), ((), ())),
                           preferred_element_type=F32)


def _split3(x):
    hi = x.astype(BF16)
    r1 = x - hi.astype(F32)
    mid = r1.astype(BF16)
    lo = (r1 - mid.astype(F32)).astype(BF16)
    return hi, mid, lo


def _dot_exact_rhs(a_bf16, x):
    hi, mid, lo = _split3(x)
    f = lambda y: jnp.dot(a_bf16, y, preferred_element_type=F32)
    return f(hi) + f(mid) + f(lo)


def _dot_exact_lhs(x, a_bf16):
    hi, mid, lo = _split3(x)
    f = lambda y: jnp.dot(y, a_bf16, preferred_element_type=F32)
    return f(hi) + f(mid) + f(lo)


def _softplus(z):
    return jnp.maximum(z, 0.0) + jnp.log(1.0 + jnp.exp(-jnp.abs(z)))


def _layer_norm_rows(x, g, b):
    mu = jnp.mean(x, axis=-1, keepdims=True)
    d = x - mu
    var = jnp.mean(d * d, axis=-1, keepdims=True)
    return d * lax.rsqrt(var + LN_EPS) * g + b


def _ada_kernel(c_ref, w_ref, b_ref, o_ref):
    c = c_ref[...]
    w = w_ref[0]
    ch, cm, cl = _split3(c)
    wh, wm, wl = _split3(w)
    f = lambda a, b: jnp.dot(a, b, preferred_element_type=F32)
    acc = f(ch, wh) + (f(ch, wm) + f(cm, wh)) + (f(ch, wl) + f(cm, wm) + f(cl, wh))
    o_ref[0] = acc + b_ref[0]


def _ada_call(c8, w_ada, b_ada):
    depth, d, d3 = w_ada.shape
    return pl.pallas_call(
        _ada_kernel,
        name="ada",
        out_shape=jax.ShapeDtypeStruct((depth, SUBLANES, d3), F32),
        grid=(depth,),
        in_specs=[pl.BlockSpec((SUBLANES, d), lambda l: (0, 0)),
                  pl.BlockSpec((1, d, d3), lambda l: (l, 0, 0)),
                  pl.BlockSpec((1, 1, d3), lambda l: (l, 0, 0))],
        out_specs=pl.BlockSpec((1, SUBLANES, d3), lambda l: (l, 0, 0)),
        compiler_params=_params(("arbitrary",)),
    )(c8, w_ada, b_ada.reshape(depth, 1, d3))


def _ln_kernel(x_ref, g_ref, b_ref, o_ref):
    o_ref[0] = _layer_norm_rows(x_ref[0], g_ref[...], b_ref[...])


def _ln_call(x, g, b, tm):
    bsz, t, d = x.shape
    return pl.pallas_call(
        _ln_kernel,
        name="emb_ln",
        out_shape=jax.ShapeDtypeStruct(x.shape, F32),
        grid=(bsz, t // tm),
        in_specs=[pl.BlockSpec((1, tm, d), lambda i, j: (i, j, 0)),
                  pl.BlockSpec((1, d), lambda i, j: (0, 0)),
                  pl.BlockSpec((1, d), lambda i, j: (0, 0))],
        out_specs=pl.BlockSpec((1, tm, d), lambda i, j: (i, j, 0)),
        compiler_params=_params(("arbitrary", "arbitrary")),
    )(x, g.reshape(1, d), b.reshape(1, d))


def _proj_kernel(x_ref, sc_ref, sh_ref, wrw_ref, wq_ref, wkt_ref, wv_ref, wg_ref, wf_ref, bf_ref,
                 selq_ref, onesq_ref, selk_ref, onesk_ref,
                 rw_ref, q_ref, kt_ref, v_ref, g_ref, qb_ref, kbt_ref, carry_ref):
    tm = x_ref.shape[1]

    @pl.when(pl.program_id(1) == 0)
    def _():
        carry_ref[...] = jnp.zeros_like(carry_ref)

    h = x_ref[0] * (1.0 + sc_ref[0]) + sh_ref[0]
    hb = h.astype(BF16)
    rw_ref[0] = jnp.dot(hb, wrw_ref[...], preferred_element_type=F32)
    g_ref[0] = jnp.dot(hb, wg_ref[...], preferred_element_type=F32)
    q_ref[0] = jnp.dot(hb, wq_ref[...], preferred_element_type=F32).astype(BF16)
    v_ref[0] = jnp.dot(hb, wv_ref[...], preferred_element_type=F32).astype(BF16)
    nt = lambda a, b: lax.dot_general(a, b, (((1,), (1,)), ((), ())), preferred_element_type=F32)
    kt_ref[0, 0] = nt(wkt_ref[...], hb).astype(BF16)

    z = jnp.dot(hb, wf_ref[...], preferred_element_type=F32) + bf_ref[...]
    lf = -_softplus(-z) * LOG2E
    r = lax.broadcasted_iota(jnp.int32, (tm, tm), 0)
    c = lax.broadcasted_iota(jnp.int32, (tm, tm), 1)
    tri = jnp.where(r >= c, 1.0, 0.0).astype(BF16)
    cum = _dot_exact_rhs(tri, lf) + carry_ref[...]
    carry_ref[...] = cum[tm - 1:tm, :]

    pos = _split3(cum)
    neg = _split3(-cum)
    qb = onesq_ref[...]
    kbt = onesk_ref[...]
    for i in range(N_SPLIT):
        qb = qb + jnp.dot(pos[i], selq_ref[i], preferred_element_type=F32)
        kbt = kbt + nt(selk_ref[i], neg[i])
    qb_ref[0] = qb.astype(BF16)
    kbt_ref[0, 0] = kbt.astype(BF16)


def _proj_call(x, scale, shift, w_rw, w_q, w_kt, w_v, w_g, w_f, bf, selq, onesq, selk, onesk, tm):
    bsz, t, d = x.shape
    nt = t // tm
    n_rw, n_g, width = w_rw.shape[1], w_g.shape[1], w_q.shape[1]
    full = lambda a: pl.BlockSpec(a.shape, lambda i, j: (0,) * a.ndim)
    rows = lambda n: pl.BlockSpec((1, tm, n), lambda i, j: (i, j, 0))
    vec = pl.BlockSpec((1, 1, d), lambda i, j: (i, 0, 0))
    tiled = lambda n: pl.BlockSpec((1, 1, n, tm), lambda i, j: (i, j, 0, 0))
    return pl.pallas_call(
        _proj_kernel,
        name="proj",
        out_shape=(jax.ShapeDtypeStruct((bsz, t, n_rw), F32),
                   jax.ShapeDtypeStruct((bsz, t, width), BF16),
                   jax.ShapeDtypeStruct((bsz, nt, width, tm), BF16),
                   jax.ShapeDtypeStruct((bsz, t, width), BF16),
                   jax.ShapeDtypeStruct((bsz, t, n_g), F32),
                   jax.ShapeDtypeStruct((bsz, t, width), BF16),
                   jax.ShapeDtypeStruct((bsz, nt, width, tm), BF16)),
        grid=(bsz, nt),
        in_specs=[rows(d), vec, vec, full(w_rw), full(w_q), full(w_kt), full(w_v), full(w_g),
                  full(w_f), full(bf), full(selq), full(onesq), full(selk), full(onesk)],
        out_specs=(rows(n_rw), rows(width), tiled(width), rows(width), rows(n_g), rows(width),
                   tiled(width)),
        scratch_shapes=[pltpu.VMEM((1, LANES), F32)],
        compiler_params=_params(("arbitrary", "arbitrary")),
    )(x, scale, shift, w_rw, w_q, w_kt, w_v, w_g, w_f, bf, selq, onesq, selk, onesk)


_P_MIX_R, _P_MIX_K, _P_MIX_V, _P_W0, _P_A0, _P_KK, _P_KA, _P_RK, _P_GNG, _P_GNB = range(10)
_P_ROWS = 16


def _rwkv_kernel(r_ref, k_ref, v_ref, wa_ref, pp_ref, mixwa_ref, wcomb_ref, o_ref,
                 carry_ref, state_ref):
    ct = r_ref.shape[1]
    n_chunks = ct // CHUNK
    C = CHUNK
    chunks = range(n_chunks)

    @pl.when(pl.program_id(2) == 0)
    def _():
        carry_ref[...] = jnp.zeros_like(carry_ref)
        state_ref[...] = jnp.zeros_like(state_ref)

    pp = pp_ref[0]
    prow = lambda i: pp[i:i + 1, :]
    row_id = lax.broadcasted_iota(jnp.int32, (ct, LANES), 0)
    lane_id = lax.broadcasted_iota(jnp.int32, (ct, LANES), 1)

    def shifted(ref, slot, mix):
        x = ref[0]
        prev = pltpu.roll(x, 1, 0)
        prev = jnp.where(row_id == 0, carry_ref[slot:slot + 1, :], prev)
        carry_ref[slot:slot + 1, :] = x[ct - 1:ct, :]
        return x + (prev - x) * mix

    r = shifted(r_ref, 0, prow(_P_MIX_R))
    k = shifted(k_ref, 1, prow(_P_MIX_K))
    v = shifted(v_ref, 2, prow(_P_MIX_V))
    wa = shifted(wa_ref, 3, mixwa_ref[...])

    t_in = jnp.where(lane_id < LORA, jnp.tanh(wa), wa)
    dwa = _dot(t_in, wcomb_ref[0])
    w_log = -_softplus(-(prow(_P_W0) + dwa[:, :LANES])) - 0.5
    lw = -jnp.exp(w_log)
    a = 1.0 / (1.0 + jnp.exp(-(prow(_P_A0) + dwa[:, LANES:])))

    bd_r = lax.broadcasted_iota(jnp.int32, (LANES, LANES), 0)
    bd_c = lax.broadcasted_iota(jnp.int32, (LANES, LANES), 1)
    same_head = (bd_r // HEAD_DIM) == (bd_c // HEAD_DIM)
    ones_bd = jnp.where(same_head, 1.0, 0.0).astype(BF16)

    kk = k * prow(_P_KK)
    ss = _dot_exact_lhs(kk * kk, ones_bd)
    kap = kk / jnp.maximum(jnp.sqrt(ss), 1e-12)
    k2 = k * (1.0 + (a - 1.0) * prow(_P_KA))
    b = kap * a
    bonus = _dot_exact_lhs(r * k2 * prow(_P_RK), ones_bd) * v

    lane_c = lax.broadcasted_iota(jnp.int32, (C, LANES), 1)
    m1 = jnp.where(lane_c < HEAD_DIM, 1.0, 0.0)
    m2 = 1.0 - m1
    stack = lambda x: jnp.concatenate([x * m1, x * m2], axis=0)
    dup = lambda x: jnp.concatenate([x, x], axis=0)
    sm = jnp.concatenate([m1, m2], axis=0)
    strict_lower = same_head & ((bd_r % C) > (bd_c % C))
    lower = same_head & ((bd_r % C) >= (bd_c % C))
    eye = bd_r == bd_c
    eye_f = jnp.where(eye, 1.0, 0.0)
    tr = lax.broadcasted_iota(jnp.int32, (C, C), 0)
    tc = lax.broadcasted_iota(jnp.int32, (C, C), 1)
    tri = jnp.where(tr >= tc, 1.0, 0.0).astype(BF16)
    gn_g, gn_b = prow(_P_GNG), prow(_P_GNB)
    sl = lambda ci: slice(ci * C, (ci + 1) * C)

    L = [_dot_exact_rhs(tri, lw[sl(ci)]) for ci in chunks]
    Lc = [L[ci][C - 1:C, :] for ci in chunks]
    e_neg = [jnp.exp(-L[ci]) for ci in chunks]
    e_end = [jnp.exp(Lc[ci] - L[ci]) for ci in chunks]
    Ks = [stack(kap[sl(ci)] * jnp.exp(L[ci] - lw[sl(ci)])).astype(BF16) for ci in chunks]
    Rs = [stack(r[sl(ci)] * jnp.exp(L[ci])) for ci in chunks]
    Vs = [stack(v[sl(ci)]).astype(BF16) for ci in chunks]
    K2 = [dup(k2[sl(ci)] * e_neg[ci]).astype(BF16) for ci in chunks]
    B2 = [dup(b[sl(ci)] * e_neg[ci]).astype(BF16) for ci in chunks]
    KendT = [stack(k2[sl(ci)] * e_end[ci]).T.astype(BF16) for ci in chunks]
    BendT = [stack(b[sl(ci)] * e_end[ci]).T.astype(BF16) for ci in chunks]
    N = [jnp.where(strict_lower, _dot_nt(Ks[ci], B2[ci]), 0.0) for ci in chunks]
    Akk = [jnp.where(strict_lower, _dot_nt(Ks[ci], K2[ci]), 0.0).astype(BF16) for ci in chunks]
    Ark = [jnp.where(lower, _dot_nt(Rs[ci], K2[ci]), 0.0).astype(BF16) for ci in chunks]
    Arb = [jnp.where(lower, _dot_nt(Rs[ci], B2[ci]), 0.0).astype(BF16) for ci in chunks]
    AkkV = [_dot(Akk[ci], Vs[ci]) for ci in chunks]
    ArkV = [_dot(Ark[ci], Vs[ci]) for ci in chunks]
    KV = [_dot(KendT[ci], Vs[ci]) for ci in chunks]
    Tm = [eye_f - N[ci] for ci in chunks]
    P = N
    for _ in range(C.bit_length() - 2):
        P = [_dot(P[ci], P[ci]) for ci in chunks]
        Tm = [Tm[ci] + _dot(Tm[ci], P[ci]) for ci in chunks]
    Tb = [Tm[ci].astype(BF16) for ci in chunks]
    P1 = [_dot(Tb[ci], Ks[ci]).astype(BF16) for ci in chunks]
    P2 = [_dot(Tb[ci], AkkV[ci]).astype(BF16) for ci in chunks]
    Q1 = [Rs[ci] - _dot(Arb[ci], P1[ci]) for ci in chunks]
    Q2 = [ArkV[ci] - _dot(Arb[ci], P2[ci]) for ci in chunks]
    G = [jnp.where(eye, jnp.broadcast_to(jnp.exp(Lc[ci]), (LANES, LANES)), 0.0)
         - _dot(BendT[ci], P1[ci]) for ci in chunks]
    H = [KV[ci] - _dot(BendT[ci], P2[ci]) for ci in chunks]

    state = state_ref[...]
    for ci in chunks:
        ys = _dot(Q1[ci], state) + Q2[ci]
        state = _dot(G[ci], state) + H[ci]
        mu = jnp.sum(ys, axis=-1, keepdims=True) * (1.0 / HEAD_DIM)
        d = (ys - mu) * sm
        var = jnp.sum(d * d, axis=-1, keepdims=True) * (1.0 / HEAD_DIM)
        yn = d * lax.rsqrt(var + GN_EPS)
        o_ref[0, sl(ci), :] = (yn[:C] + yn[C:]) * gn_g + gn_b + bonus[sl(ci)]
    state_ref[...] = state


def _rwkv_call(rw, pp, mixwa, wcomb, width, ct):
    bsz, t, _ = rw.shape
    n_pairs = width // LANES
    blk = lambda off: pl.BlockSpec((1, ct, LANES), lambda i, p, j, off=off: (i, j, off + p))
    wa_blk = pl.BlockSpec((1, ct, LANES), lambda i, p, j: (i, j, 3 * n_pairs))
    return pl.pallas_call(
        _rwkv_kernel,
        name="rwkv",
        out_shape=jax.ShapeDtypeStruct((bsz, t, width), F32),
        grid=(bsz, n_pairs, t // ct),
        in_specs=[blk(0), blk(n_pairs), blk(2 * n_pairs), wa_blk,
                  pl.BlockSpec((1, _P_ROWS, LANES), lambda i, p, j: (p, 0, 0)),
                  pl.BlockSpec((1, LANES), lambda i, p, j: (0, 0)),
                  pl.BlockSpec((1, LANES, 2 * LANES), lambda i, p, j: (p, 0, 0))],
        out_specs=pl.BlockSpec((1, ct, LANES), lambda i, p, j: (i, j, p)),
        scratch_shapes=[pltpu.VMEM((SUBLANES, LANES), F32), pltpu.VMEM((LANES, LANES), F32)],
        compiler_params=_params(("arbitrary", "arbitrary", "arbitrary")),
    )(rw, rw, rw, rw, pp, mixwa, wcomb)


def _fox_kernel(q_ref, qb_ref, kt_ref, kbt_ref, v_ref, o_ref, s_buf, p_buf, m_buf, a_buf, acc_buf):
    bq = q_ref.shape[1]
    bk = bq
    qi = pl.program_id(2)
    heads = range(PAIR)
    lane_q = lax.broadcasted_iota(jnp.int32, (bq, LANES), 1)
    lane_v = lax.broadcasted_iota(jnp.int32, (bk, LANES), 1)
    own_q = (lane_q < HEAD_DIM, lane_q >= HEAD_DIM)
    own_v = (lane_v < HEAD_DIM, lane_v >= HEAD_DIM)
    den_lane = (HEAD_DIM, 0)
    q, qb = q_ref[0], qb_ref[0]
    q_aug = [jnp.where(own_q[h], q, qb) for h in heads]
    v_ones = [jnp.where(lane_v == den_lane[h], 1.0, 0.0).astype(BF16) for h in heads]
    rr = lax.broadcasted_iota(jnp.int32, (bq, bk), 0)
    cc = lax.broadcasted_iota(jnp.int32, (bq, bk), 1)

    wide = lambda x: jnp.concatenate([x] * (bk // LANES), axis=1)

    def scores(j, slot, m_prev, causal=False):
        kt, kbt = kt_ref[0, j], kbt_ref[0, j]
        k_aug = (jnp.concatenate([kt[:HEAD_DIM], kbt[HEAD_DIM:]], axis=0),
                 jnp.concatenate([kbt[:HEAD_DIM], kt[HEAD_DIM:]], axis=0))
        for h in heads:
            s = jnp.dot(q_aug[h], k_aug[h], preferred_element_type=F32)
            if causal:
                s = jnp.where(cc <= rr, s, -jnp.inf)
            s_buf[slot, h] = s
            bmax = jnp.broadcast_to(jnp.max(s, axis=1, keepdims=True), (bq, LANES))
            m_buf[slot, h] = jnp.maximum(m_prev[h], bmax)

    def softmax(slot, m_prev):
        for h in heads:
            m_cur = m_buf[slot, h]
            a_buf[slot, h] = jnp.exp2(m_prev[h] - m_cur)
            p_buf[slot, h] = jnp.exp2(s_buf[slot, h] - wide(m_cur)).astype(BF16)

    def values(j, slot):
        vb = v_ref[0, pl.ds(pl.multiple_of(j * bk, bk), bk), :]
        for h in heads:
            v_aug = jnp.where(own_v[h], vb, v_ones[h])
            acc_buf[h] = a_buf[slot, h] * acc_buf[h] + jnp.dot(p_buf[slot, h], v_aug,
                                                               preferred_element_type=F32)

    def step(j, slot, causal=False):
        m_before = [m_buf[1 - slot, h] for h in heads]
        scores(j + 1, 1 - slot, [m_buf[slot, h] for h in heads], causal)
        values(jnp.maximum(j - 1, 0), 1 - slot)
        softmax(slot, m_before)

    def drain(slot):
        values(jnp.maximum(qi - 1, 0), 1 - slot)
        softmax(slot, [m_buf[1 - slot, h] for h in heads])
        values(qi, slot)

    p_buf[1] = jnp.zeros(p_buf.shape[1:], BF16)
    a_buf[1] = jnp.ones(a_buf.shape[1:], F32)
    m_buf[1] = jnp.full(m_buf.shape[1:], -jnp.inf, F32)
    acc_buf[...] = jnp.zeros_like(acc_buf)
    odd = qi % 2 == 1

    @pl.when(qi == 0)
    def _():
        scores(0, 0, [m_buf[1, h] for h in heads], causal=True)
        drain(0)

    @pl.when(qi > 0)
    def _():
        scores(0, 0, [m_buf[1, h] for h in heads])

    def pair(jj, carry):
        step(2 * jj, 0)
        step(2 * jj + 1, 1)
        return carry

    lax.fori_loop(0, jnp.maximum(qi - 1, 0) // 2, pair, 0)

    @pl.when(jnp.logical_and(qi >= 2, jnp.logical_not(odd)))
    def _():
        step(qi - 2, 0)

    @pl.when(odd)
    def _():
        step(qi - 1, 0, causal=True)
        drain(1)

    @pl.when(jnp.logical_and(qi >= 2, jnp.logical_not(odd)))
    def _():
        step(qi - 1, 1, causal=True)
        drain(0)

    out = [acc_buf[h] / acc_buf[h][:, den_lane[h]:den_lane[h] + 1] for h in heads]
    o_ref[0] = jnp.where(own_q[0], out[0], out[1])


def _fox_call(q, qb, kt, kbt, v):
    bsz, t, width = q.shape
    nt, bq = kt.shape[1], kt.shape[3]
    n_pairs = width // LANES
    q_blk = pl.BlockSpec((1, bq, LANES), lambda i, p, j: (i, j, p))
    kt_blk = pl.BlockSpec((1, nt, LANES, bq), lambda i, p, j: (i, 0, p, 0))
    return pl.pallas_call(
        _fox_kernel,
        name="fox",
        out_shape=jax.ShapeDtypeStruct((bsz, t, width), F32),
        grid=(bsz, n_pairs, nt),
        in_specs=[q_blk, q_blk, kt_blk, kt_blk,
                  pl.BlockSpec((1, t, LANES), lambda i, p, j: (i, 0, p))],
        out_specs=q_blk,
        scratch_shapes=[pltpu.VMEM((2, PAIR, bq, bq), F32), pltpu.VMEM((2, PAIR, bq, bq), BF16),
                        pltpu.VMEM((2, PAIR, bq, LANES), F32), pltpu.VMEM((2, PAIR, bq, LANES), F32),
                        pltpu.VMEM((PAIR, bq, LANES), F32)],
        compiler_params=_params(("arbitrary", "arbitrary", "arbitrary")),
    )(q, qb, kt, kbt, v)


def _merge_kernel(ya_ref, yb_ref, g_ref, x_ref, gc_ref, wa_ref, wb_ref, lg_ref, lb_ref, o_ref,
                  *, alpha):
    half = ya_ref.shape[2]
    g = g_ref[0]
    silu = g / (1.0 + jnp.exp(-g))
    out = (_dot(ya_ref[0] * silu[:, :half], wa_ref[...])
           + _dot(yb_ref[0] * silu[:, half:], wb_ref[...]))
    z = alpha * x_ref[0] + (1.0 + gc_ref[0]) * out
    o_ref[0] = _layer_norm_rows(z, lg_ref[...], lb_ref[...])


def _merge_call(ya, yb, g, x, gate_c, w_a, w_b, ln_g, ln_b, alpha, tm):
    bsz, t, d = x.shape
    half = ya.shape[2]
    rows = lambda n: pl.BlockSpec((1, tm, n), lambda i, j: (i, j, 0))
    full = lambda a: pl.BlockSpec(a.shape, lambda i, j: (0,) * a.ndim)
    return pl.pallas_call(
        functools.partial(_merge_kernel, alpha=alpha),
        name="merge",
        out_shape=jax.ShapeDtypeStruct(x.shape, F32),
        grid=(bsz, t // tm),
        in_specs=[rows(half), rows(half), rows(g.shape[2]), rows(d),
                  pl.BlockSpec((1, 1, d), lambda i, j: (i, 0, 0)),
                  full(w_a), full(w_b), full(ln_g), full(ln_b)],
        out_specs=rows(d),
        compiler_params=_params(("arbitrary", "arbitrary")),
    )(ya, yb, g, x, gate_c, w_a, w_b, ln_g, ln_b)


def _layer_tables(rwkv_mix, w0, w_up, a0, a_up, k_k, k_a, r_k, gn_g, gn_b, width):
    n_pairs = width // LANES
    per_pair = lambda vct: vct.reshape(n_pairs, 1, LANES)
    rows = [rwkv_mix[0:width], rwkv_mix[width:2 * width], rwkv_mix[2 * width:3 * width],
            w0, a0, k_k, k_a, r_k, gn_g, gn_b]
    pp = jnp.concatenate([per_pair(x) for x in rows]
                         + [jnp.zeros((n_pairs, _P_ROWS - len(rows), LANES), F32)], axis=1)
    mixwa = rwkv_mix[3 * width:].reshape(1, 2 * LORA)
    zeros = jnp.zeros((n_pairs, LORA, LANES), F32)
    wu = w_up.reshape(LORA, n_pairs, LANES).transpose(1, 0, 2)
    au = a_up.reshape(LORA, n_pairs, LANES).transpose(1, 0, 2)
    wcomb = jnp.concatenate([jnp.concatenate([wu, zeros], axis=2),
                             jnp.concatenate([zeros, au], axis=2)], axis=1).astype(BF16)
    return pp, mixwa, wcomb


def _fox_tables(heads, tm):
    width = heads * HEAD_DIM
    chan = jnp.arange(width)
    off = chan % LANES
    src = jnp.arange(LANES)[:, None]
    base = lambda hd: (hd // PAIR) * LANES + jnp.where(hd % PAIR == 0, HEAD_DIM, 0)
    at = lambda slot: (chan[None, :] == base(src) + slot) & (src < heads)
    selq = jnp.stack([at(N_SPLIT + i) for i in range(N_SPLIT)]).astype(BF16)
    selk = jnp.stack([at(i).T for i in range(N_SPLIT)]).astype(BF16)
    slot_of = off % HEAD_DIM
    onesq = (slot_of < N_SPLIT).astype(F32).reshape(1, width)
    onesk = jnp.broadcast_to(((slot_of >= N_SPLIT) & (slot_of < 2 * N_SPLIT))
                             .astype(F32)[:, None], (width, tm))
    return selq, onesq, selk, onesk


def kernel(x, c, emb_ln_g, emb_ln_b, w_ada, b_ada, w_in, rwkv_mix, w0, w_up, a0, a_up, k_k, k_a,
           r_k, gn_g, gn_b, fox_bf, w_out, ln_g, ln_b):
    bsz, t, d = x.shape
    depth = w_ada.shape[0]
    width = d // 2
    heads = width // HEAD_DIM
    rw_end = 3 * width + 2 * LORA
    fx_end = rw_end + 3 * width + heads
    alpha = (2 * depth) ** 0.25
    tm = min(TILE, t)

    c8 = jnp.pad(c, ((0, SUBLANES - bsz), (0, 0)))
    mods = _ada_call(c8, w_ada, b_ada)[:, :bsz]
    selq, onesq, selk, onesk = _fox_tables(heads, tm)

    h = _ln_call(x, emb_ln_g, emb_ln_b, tm)
    for l in range(depth):
        shift = mods[l, :, 0:d].reshape(bsz, 1, d)
        scale = mods[l, :, d:2 * d].reshape(bsz, 1, d)
        gate_c = mods[l, :, 2 * d:].reshape(bsz, 1, d)
        w = w_in[l]
        w_rw = w[:, :rw_end].astype(BF16)
        w_q = (w[:, rw_end:rw_end + width] * (HEAD_DIM ** -0.5 * LOG2E)).astype(BF16)
        w_kt = w[:, rw_end + width:rw_end + 2 * width].T.astype(BF16)
        w_v = w[:, rw_end + 2 * width:rw_end + 3 * width].astype(BF16)
        w_f = jnp.pad(w[:, rw_end + 3 * width:fx_end], ((0, 0), (0, LANES - heads))).astype(BF16)
        w_g = w[:, fx_end:].astype(BF16)
        bf = jnp.pad(fox_bf[l], (0, LANES - heads)).reshape(1, LANES)
        rw, q, kt, v, g, qb, kbt = _proj_call(h, scale, shift, w_rw, w_q, w_kt, w_v, w_g, w_f, bf,
                                              selq, onesq, selk, onesk, tm)

        pp, mixwa, wcomb = _layer_tables(rwkv_mix[l], w0[l], w_up[l], a0[l], a_up[l], k_k[l],
                                         k_a[l], r_k[l], gn_g[l], gn_b[l], width)
        ya = _rwkv_call(rw, pp, mixwa, wcomb, width, tm)
        yb = _fox_call(q, qb, kt, kbt, v)
        wo = w_out[l].astype(BF16)
        h = _merge_call(ya, yb, g, h, gate_c, wo[:width], wo[width:], ln_g[l].reshape(1, d),
                        ln_b[l].reshape(1, d), alpha, tm)
    return h
```

```python
import functools

import jax
import jax.numpy as jnp
from jax import lax
from jax.experimental import pallas as pl
from jax.experimental.pallas import tpu as pltpu

F32 = jnp.float32
BF16 = jnp.bfloat16

HEAD_DIM = 64
LANES = 128
SUBLANES = 8
PAIR = LANES // HEAD_DIM
LORA = 64
LN_EPS = 1e-5
GN_EPS = 64e-5
CHUNK = 64
TILE = 512
N_SPLIT = 3
FOX_Q_BLOCKS = 2
LOG2E = 1.4426950408889634
VMEM_LIMIT = 56 * 1024 * 1024


def _params(sem):
    return pltpu.CompilerParams(dimension_semantics=sem, vmem_limit_bytes=VMEM_LIMIT)


def _dot(a, b):
    return jnp.dot(a.astype(BF16), b.astype(BF16), preferred_element_type=F32)


def _dot_nt(a, b):
    return lax.dot_general(a.astype(BF16), b.astype(BF16), (((1,), (1,)), ((), ())),
                           preferred_element_type=F32)


def _split3(x):
    hi = x.astype(BF16)
    r1 = x - hi.astype(F32)
    mid = r1.astype(BF16)
    lo = (r1 - mid.astype(F32)).astype(BF16)
    return hi, mid, lo


def _dot_exact_rhs(a_bf16, x):
    hi, mid, lo = _split3(x)
    f = lambda y: jnp.dot(a_bf16, y, preferred_element_type=F32)
    return f(hi) + f(mid) + f(lo)


def _dot_exact_lhs(x, a_bf16):
    hi, mid, lo = _split3(x)
    f = lambda y: jnp.dot(y, a_bf16, preferred_element_type=F32)
    return f(hi) + f(mid) + f(lo)


def _softplus(z):
    return jnp.maximum(z, 0.0) + jnp.log(1.0 + jnp.exp(-jnp.abs(z)))


def _layer_norm_rows(x, g, b):
    mu = jnp.mean(x, axis=-1, keepdims=True)
    d = x - mu
    var = jnp.mean(d * d, axis=-1, keepdims=True)
    return d * lax.rsqrt(var + LN_EPS) * g + b


def _ada_kernel(c_ref, w_ref, b_ref, o_ref):
    c = c_ref[...]
    w = w_ref[0]
    ch, cm, cl = _split3(c)
    wh, wm, wl = _split3(w)
    f = lambda a, b: jnp.dot(a, b, preferred_element_type=F32)
    acc = f(ch, wh) + (f(ch, wm) + f(cm, wh)) + (f(ch, wl) + f(cm, wm) + f(cl, wh))
    o_ref[0] = acc + b_ref[0]


def _ada_call(c8, w_ada, b_ada):
    depth, d, d3 = w_ada.shape
    return pl.pallas_call(
        _ada_kernel,
        name="ada",
        out_shape=jax.ShapeDtypeStruct((depth, SUBLANES, d3), F32),
        grid=(depth,),
        in_specs=[pl.BlockSpec((SUBLANES, d), lambda l: (0, 0)),
                  pl.BlockSpec((1, d, d3), lambda l: (l, 0, 0)),
                  pl.BlockSpec((1, 1, d3), lambda l: (l, 0, 0))],
        out_specs=pl.BlockSpec((1, SUBLANES, d3), lambda l: (l, 0, 0)),
        compiler_params=_params(("arbitrary",)),
    )(c8, w_ada, b_ada.reshape(depth, 1, d3))


def _ln_kernel(x_ref, g_ref, b_ref, o_ref):
    o_ref[0] = _layer_norm_rows(x_ref[0], g_ref[...], b_ref[...])


def _ln_call(x, g, b, tm):
    bsz, t, d = x.shape
    return pl.pallas_call(
        _ln_kernel,
        name="emb_ln",
        out_shape=jax.ShapeDtypeStruct(x.shape, F32),
        grid=(bsz, t // tm),
        in_specs=[pl.BlockSpec((1, tm, d), lambda i, j: (i, j, 0)),
                  pl.BlockSpec((1, d), lambda i, j: (0, 0)),
                  pl.BlockSpec((1, d), lambda i, j: (0, 0))],
        out_specs=pl.BlockSpec((1, tm, d), lambda i, j: (i, j, 0)),
        compiler_params=_params(("arbitrary", "arbitrary")),
    )(x, g.reshape(1, d), b.reshape(1, d))


def _proj_kernel(x_ref, sc_ref, sh_ref, wrw_ref, wq_ref, wkt_ref, wv_ref, wg_ref, wf_ref, bf_ref,
                 selq_ref, onesq_ref, selk_ref, onesk_ref,
                 rw_ref, q_ref, kt_ref, v_ref, g_ref, qb_ref, kbt_ref, carry_ref):
    tm = x_ref.shape[1]

    @pl.when(pl.program_id(1) == 0)
    def _():
        carry_ref[...] = jnp.zeros_like(carry_ref)

    h = x_ref[0] * (1.0 + sc_ref[0]) + sh_ref[0]
    hb = h.astype(BF16)
    rw_ref[0] = jnp.dot(hb, wrw_ref[...], preferred_element_type=F32)
    g_ref[0] = jnp.dot(hb, wg_ref[...], preferred_element_type=F32)
    q_ref[0] = jnp.dot(hb, wq_ref[...], preferred_element_type=F32).astype(BF16)
    v_ref[0] = jnp.dot(hb, wv_ref[...], preferred_element_type=F32).astype(BF16)
    nt = lambda a, b: lax.dot_general(a, b, (((1,), (1,)), ((), ())), preferred_element_type=F32)
    kt_ref[0, 0] = nt(wkt_ref[...], hb).astype(BF16)

    z = jnp.dot(hb, wf_ref[...], preferred_element_type=F32) + bf_ref[...]
    lf = -_softplus(-z) * LOG2E
    r = lax.broadcasted_iota(jnp.int32, (tm, tm), 0)
    c = lax.broadcasted_iota(jnp.int32, (tm, tm), 1)
    tri = jnp.where(r >= c, 1.0, 0.0).astype(BF16)
    cum = _dot_exact_rhs(tri, lf) + carry_ref[...]
    carry_ref[...] = cum[tm - 1:tm, :]

    pos = _split3(cum)
    neg = _split3(-cum)
    qb = onesq_ref[...]
    kbt = onesk_ref[...]
    for i in range(N_SPLIT):
        qb = qb + jnp.dot(pos[i], selq_ref[i], preferred_element_type=F32)
        kbt = kbt + nt(selk_ref[i], neg[i])
    qb_ref[0] = qb.astype(BF16)
    kbt_ref[0, 0] = kbt.astype(BF16)


def _proj_call(x, scale, shift, w_rw, w_q, w_kt, w_v, w_g, w_f, bf, selq, onesq, selk, onesk, tm):
    bsz, t, d = x.shape
    nt = t // tm
    n_rw, n_g, width = w_rw.shape[1], w_g.shape[1], w_q.shape[1]
    full = lambda a: pl.BlockSpec(a.shape, lambda i, j: (0,) * a.ndim)
    rows = lambda n: pl.BlockSpec((1, tm, n), lambda i, j: (i, j, 0))
    vec = pl.BlockSpec((1, 1, d), lambda i, j: (i, 0, 0))
    tiled = lambda n: pl.BlockSpec((1, 1, n, tm), lambda i, j: (i, j, 0, 0))
    return pl.pallas_call(
        _proj_kernel,
        name="proj",
        out_shape=(jax.ShapeDtypeStruct((bsz, t, n_rw), F32),
                   jax.ShapeDtypeStruct((bsz, t, width), BF16),
                   jax.ShapeDtypeStruct((bsz, nt, width, tm), BF16),
                   jax.ShapeDtypeStruct((bsz, t, width), BF16),
                   jax.ShapeDtypeStruct((bsz, t, n_g), F32),
                   jax.ShapeDtypeStruct((bsz, t, width), BF16),
                   jax.ShapeDtypeStruct((bsz, nt, width, tm), BF16)),
        grid=(bsz, nt),
        in_specs=[rows(d), vec, vec, full(w_rw), full(w_q), full(w_kt), full(w_v), full(w_g),
                  full(w_f), full(bf), full(selq), full(onesq), full(selk), full(onesk)],
        out_specs=(rows(n_rw), rows(width), tiled(width), rows(width), rows(n_g), rows(width),
                   tiled(width)),
        scratch_shapes=[pltpu.VMEM((1, LANES), F32)],
        compiler_params=_params(("arbitrary", "arbitrary")),
    )(x, scale, shift, w_rw, w_q, w_kt, w_v, w_g, w_f, bf, selq, onesq, selk, onesk)


_P_MIX_R, _P_MIX_K, _P_MIX_V, _P_W0, _P_A0, _P_KK, _P_KA, _P_RK, _P_GNG, _P_GNB = range(10)
_P_ROWS = 16


def _rwkv_kernel(r_ref, k_ref, v_ref, wa_ref, pp_ref, mixwa_ref, wcomb_ref, o_ref,
                 carry_ref, state_ref, q1_s, q2_s, g_s, h_s, bonus_s):
    ct = r_ref.shape[1]
    n_chunks = ct // CHUNK
    C = CHUNK
    chunks = range(n_chunks)

    @pl.when(pl.program_id(2) == 0)
    def _():
        for ref in (carry_ref, state_ref, q1_s, q2_s, g_s, h_s, bonus_s):
            ref[...] = jnp.zeros_like(ref)

    pp = pp_ref[0]
    prow = lambda i: pp[i:i + 1, :]
    row_id = lax.broadcasted_iota(jnp.int32, (ct, LANES), 0)
    lane_id = lax.broadcasted_iota(jnp.int32, (ct, LANES), 1)

    def shifted(ref, slot, mix):
        x = ref[0]
        prev = pltpu.roll(x, 1, 0)
        prev = jnp.where(row_id == 0, carry_ref[slot:slot + 1, :], prev)
        carry_ref[slot:slot + 1, :] = x[ct - 1:ct, :]
        return x + (prev - x) * mix

    r = shifted(r_ref, 0, prow(_P_MIX_R))
    k = shifted(k_ref, 1, prow(_P_MIX_K))
    v = shifted(v_ref, 2, prow(_P_MIX_V))
    wa = shifted(wa_ref, 3, mixwa_ref[...])

    t_in = jnp.where(lane_id < LORA, jnp.tanh(wa), wa)
    dwa = _dot(t_in, wcomb_ref[0])
    w_log = -_softplus(-(prow(_P_W0) + dwa[:, :LANES])) - 0.5
    lw = -jnp.exp(w_log)
    a = 1.0 / (1.0 + jnp.exp(-(prow(_P_A0) + dwa[:, LANES:])))

    bd_r = lax.broadcasted_iota(jnp.int32, (LANES, LANES), 0)
    bd_c = lax.broadcasted_iota(jnp.int32, (LANES, LANES), 1)
    same_head = (bd_r // HEAD_DIM) == (bd_c // HEAD_DIM)
    ones_bd = jnp.where(same_head, 1.0, 0.0).astype(BF16)

    kk = k * prow(_P_KK)
    ss = _dot_exact_lhs(kk * kk, ones_bd)
    kap = kk / jnp.maximum(jnp.sqrt(ss), 1e-12)
    k2 = k * (1.0 + (a - 1.0) * prow(_P_KA))
    b = kap * a
    bonus = _dot_exact_lhs(r * k2 * prow(_P_RK), ones_bd) * v

    lane_c = lax.broadcasted_iota(jnp.int32, (C, LANES), 1)
    m1 = jnp.where(lane_c < HEAD_DIM, 1.0, 0.0)
    m2 = 1.0 - m1
    stack = lambda x: jnp.concatenate([x * m1, x * m2], axis=0)
    dup = lambda x: jnp.concatenate([x, x], axis=0)
    sm = jnp.concatenate([m1, m2], axis=0)
    strict_lower = same_head & ((bd_r % C) > (bd_c % C))
    lower = same_head & ((bd_r % C) >= (bd_c % C))
    eye = bd_r == bd_c
    eye_f = jnp.where(eye, 1.0, 0.0)
    tr = lax.broadcasted_iota(jnp.int32, (C, C), 0)
    tc = lax.broadcasted_iota(jnp.int32, (C, C), 1)
    tri = jnp.where(tr >= tc, 1.0, 0.0).astype(BF16)
    gn_g, gn_b = prow(_P_GNG), prow(_P_GNB)
    sl = lambda ci: slice(ci * C, (ci + 1) * C)

    state = [state_ref[...]]

    def emit_previous(ci):
        ys = _dot(q1_s[ci], state[0]) + q2_s[ci]
        state[0] = _dot(g_s[ci], state[0]) + h_s[ci]
        mu = jnp.sum(ys, axis=-1, keepdims=True) * (1.0 / HEAD_DIM)
        d = (ys - mu) * sm
        var = jnp.sum(d * d, axis=-1, keepdims=True) * (1.0 / HEAD_DIM)
        yn = d * lax.rsqrt(var + GN_EPS)
        o_ref[0, sl(ci), :] = (yn[:C] + yn[C:]) * gn_g + gn_b + bonus_s[sl(ci), :]

    pending = list(chunks)

    def hook():
        if pending:
            emit_previous(pending.pop(0))

    L = [_dot_exact_rhs(tri, lw[sl(ci)]) for ci in chunks]
    Lc = [x[C - 1:C, :] for x in L]
    e_neg = [jnp.exp(-x) for x in L]
    e_end = [jnp.exp(xc - x) for xc, x in zip(Lc, L)]
    Ks = [stack(kap[sl(ci)] * jnp.exp(L[ci] - lw[sl(ci)])).astype(BF16) for ci in chunks]
    Rs = [stack(r[sl(ci)] * jnp.exp(L[ci])) for ci in chunks]
    Vs = [stack(v[sl(ci)]).astype(BF16) for ci in chunks]
    K2 = [dup(k2[sl(ci)] * e_neg[ci]).astype(BF16) for ci in chunks]
    B2 = [dup(b[sl(ci)] * e_neg[ci]).astype(BF16) for ci in chunks]
    KendT = [stack(k2[sl(ci)] * e_end[ci]).T.astype(BF16) for ci in chunks]
    BendT = [stack(b[sl(ci)] * e_end[ci]).T.astype(BF16) for ci in chunks]
    N = [jnp.where(strict_lower, _dot_nt(Ks[ci], B2[ci]), 0.0) for ci in chunks]
    Akk = [jnp.where(strict_lower, _dot_nt(Ks[ci], K2[ci]), 0.0).astype(BF16) for ci in chunks]
    hook()
    Ark = [jnp.where(lower, _dot_nt(Rs[ci], K2[ci]), 0.0).astype(BF16) for ci in chunks]
    Arb = [jnp.where(lower, _dot_nt(Rs[ci], B2[ci]), 0.0).astype(BF16) for ci in chunks]
    hook()
    AkkV = [_dot(Akk[ci], Vs[ci]) for ci in chunks]
    ArkV = [_dot(Ark[ci], Vs[ci]) for ci in chunks]
    KV = [_dot(KendT[ci], Vs[ci]) for ci in chunks]
    hook()
    P = [_dot(x, x) for x in N]
    T = [eye_f - x for x in N]
    for _ in range(C.bit_length() - 3):
        both = [_dot(jnp.concatenate([t, p], axis=0), p) for t, p in zip(T, P)]
        T = [t + x[:2 * C] for t, x in zip(T, both)]
        P = [x[2 * C:] for x in both]
        hook()
    Tb = [(t + _dot(t, p)).astype(BF16) for t, p in zip(T, P)]
    hook()
    P1 = [_dot(Tb[ci], Ks[ci]).astype(BF16) for ci in chunks]
    P2 = [_dot(Tb[ci], AkkV[ci]).astype(BF16) for ci in chunks]
    while pending:
        hook()
    state_ref[...] = state[0]
    for ci in chunks:
        q1_s[ci] = Rs[ci] - _dot(Arb[ci], P1[ci])
        q2_s[ci] = ArkV[ci] - _dot(Arb[ci], P2[ci])
        g_s[ci] = (jnp.where(eye, jnp.broadcast_to(jnp.exp(Lc[ci]), (LANES, LANES)), 0.0)
                   - _dot(BendT[ci], P1[ci]))
        h_s[ci] = KV[ci] - _dot(BendT[ci], P2[ci])
    bonus_s[...] = bonus


def _rwkv_call(rw, pp, mixwa, wcomb, width, ct):
    bsz, t, _ = rw.shape
    n_pairs = width // LANES
    nt = t // ct
    last = nt - 1
    blk = lambda off: pl.BlockSpec((1, ct, LANES),
                                   lambda i, p, j, off=off: (i, jnp.minimum(j, last), off + p))
    wa_blk = pl.BlockSpec((1, ct, LANES), lambda i, p, j: (i, jnp.minimum(j, last), 3 * n_pairs))
    staged = pltpu.VMEM((ct // CHUNK, LANES, LANES), F32)
    return pl.pallas_call(
        _rwkv_kernel,
        name="rwkv",
        out_shape=jax.ShapeDtypeStruct((bsz, t, width), F32),
        grid=(bsz, n_pairs, nt + 1),
        in_specs=[blk(0), blk(n_pairs), blk(2 * n_pairs), wa_blk,
                  pl.BlockSpec((1, _P_ROWS, LANES), lambda i, p, j: (p, 0, 0)),
                  pl.BlockSpec((1, LANES), lambda i, p, j: (0, 0)),
                  pl.BlockSpec((1, LANES, 2 * LANES), lambda i, p, j: (p, 0, 0))],
        out_specs=pl.BlockSpec((1, ct, LANES), lambda i, p, j: (i, jnp.maximum(j - 1, 0), p)),
        scratch_shapes=[pltpu.VMEM((SUBLANES, LANES), F32), pltpu.VMEM((LANES, LANES), F32),
                        staged, staged, staged, staged, pltpu.VMEM((ct, LANES), F32)],
        compiler_params=_params(("arbitrary", "arbitrary", "arbitrary")),
    )(rw, rw, rw, rw, pp, mixwa, wcomb)


def _fox_kernel(q_ref, qb_ref, kt_ref, kbt_ref, v_ref, o_ref, s_buf, p_buf, m_buf, a_buf, acc_buf):
    bq, bk = q_ref.shape[1], kt_ref.shape[3]
    qi = pl.program_id(2)
    n_full = qi * FOX_Q_BLOCKS
    heads = range(PAIR)
    lane_q = lax.broadcasted_iota(jnp.int32, (bq, LANES), 1)
    lane_v = lax.broadcasted_iota(jnp.int32, (bk, LANES), 1)
    own_q = (lane_q < HEAD_DIM, lane_q >= HEAD_DIM)
    own_v = (lane_v < HEAD_DIM, lane_v >= HEAD_DIM)
    den_lane = (HEAD_DIM, 0)
    q, qb = q_ref[0], qb_ref[0]
    q_aug = [jnp.where(own_q[h], q, qb) for h in heads]
    v_ones = [jnp.where(lane_v == den_lane[h], 1.0, 0.0).astype(BF16) for h in heads]
    rr = lax.broadcasted_iota(jnp.int32, (bq, bk), 0)
    cc = lax.broadcasted_iota(jnp.int32, (bq, bk), 1)

    wide = lambda x: jnp.concatenate([x] * (bk // LANES), axis=1)

    def scores(j, slot, m_prev, diag=None):
        kt, kbt = kt_ref[0, j], kbt_ref[0, j]
        k_aug = (jnp.concatenate([kt[:HEAD_DIM], kbt[HEAD_DIM:]], axis=0),
                 jnp.concatenate([kbt[:HEAD_DIM], kt[HEAD_DIM:]], axis=0))
        for h in heads:
            s = jnp.dot(q_aug[h], k_aug[h], preferred_element_type=F32)
            if diag is not None:
                s = jnp.where(cc + diag * bk <= rr, s, -jnp.inf)
            s_buf[slot, h] = s
            bmax = jnp.broadcast_to(jnp.max(s, axis=1, keepdims=True), (bq, LANES))
            m_buf[slot, h] = jnp.maximum(m_prev[h], bmax)

    def softmax(slot, m_prev):
        for h in heads:
            m_cur = m_buf[slot, h]
            a_buf[slot, h] = jnp.exp2(m_prev[h] - m_cur)
            p_buf[slot, h] = jnp.exp2(s_buf[slot, h] - wide(m_cur)).astype(BF16)

    def values(j, slot):
        vb = v_ref[0, pl.ds(pl.multiple_of(j * bk, bk), bk), :]
        for h in heads:
            v_aug = jnp.where(own_v[h], vb, v_ones[h])
            acc_buf[h] = a_buf[slot, h] * acc_buf[h] + jnp.dot(p_buf[slot, h], v_aug,
                                                               preferred_element_type=F32)

    def step(j, slot, diag=None):
        m_before = [m_buf[1 - slot, h] for h in heads]
        scores(j + 1, 1 - slot, [m_buf[slot, h] for h in heads], diag)
        values(jnp.maximum(j - 1, 0), 1 - slot)
        softmax(slot, m_before)

    p_buf[1] = jnp.zeros(p_buf.shape[1:], BF16)
    a_buf[1] = jnp.ones(a_buf.shape[1:], F32)
    m_buf[1] = jnp.full(m_buf.shape[1:], -jnp.inf, F32)
    acc_buf[...] = jnp.zeros_like(acc_buf)

    @pl.when(qi == 0)
    def _():
        scores(0, 0, [m_buf[1, h] for h in heads], diag=0)

    @pl.when(qi > 0)
    def _():
        scores(0, 0, [m_buf[1, h] for h in heads])

    def pair(jj, carry):
        step(2 * jj, 0)
        step(2 * jj + 1, 1)
        return carry

    lax.fori_loop(0, jnp.maximum(qi - 1, 0), pair, 0)

    @pl.when(qi > 0)
    def _():
        step(n_full - 2, 0)
        step(n_full - 1, 1, diag=0)

    step(n_full, 0, diag=1)
    values(n_full, 0)
    softmax(1, [m_buf[0, h] for h in heads])
    values(n_full + 1, 1)
    out = [acc_buf[h] / acc_buf[h][:, den_lane[h]:den_lane[h] + 1] for h in heads]
    o_ref[0] = jnp.where(own_q[0], out[0], out[1])


def _fox_call(q, qb, kt, kbt, v):
    bsz, t, width = q.shape
    nt, bk = kt.shape[1], kt.shape[3]
    bq = FOX_Q_BLOCKS * bk
    n_pairs = width // LANES
    q_blk = pl.BlockSpec((1, bq, LANES), lambda i, p, j: (i, j, p))
    once = pl.Buffered(1)
    kt_blk = pl.BlockSpec((1, nt, LANES, bk), lambda i, p, j: (i, 0, p, 0), pipeline_mode=once)
    v_blk = pl.BlockSpec((1, t, LANES), lambda i, p, j: (i, 0, p), pipeline_mode=once)
    return pl.pallas_call(
        _fox_kernel,
        name="fox",
        out_shape=jax.ShapeDtypeStruct((bsz, t, width), F32),
        grid=(bsz, n_pairs, t // bq),
        in_specs=[q_blk, q_blk, kt_blk, kt_blk, v_blk],
        out_specs=q_blk,
        scratch_shapes=[pltpu.VMEM((2, PAIR, bq, bk), F32), pltpu.VMEM((2, PAIR, bq, bk), BF16),
                        pltpu.VMEM((2, PAIR, bq, LANES), F32), pltpu.VMEM((2, PAIR, bq, LANES), F32),
                        pltpu.VMEM((PAIR, bq, LANES), F32)],
        compiler_params=_params(("arbitrary", "arbitrary", "arbitrary")),
    )(q, qb, kt, kbt, v)


def _merge_kernel(ya_ref, yb_ref, g_ref, x_ref, gc_ref, wa_ref, wb_ref, lg_ref, lb_ref, o_ref,
                  *, alpha):
    half = ya_ref.shape[2]
    g = g_ref[0]
    silu = g / (1.0 + jnp.exp(-g))
    out = (_dot(ya_ref[0] * silu[:, :half], wa_ref[...])
           + _dot(yb_ref[0] * silu[:, half:], wb_ref[...]))
    z = alpha * x_ref[0] + (1.0 + gc_ref[0]) * out
    o_ref[0] = _layer_norm_rows(z, lg_ref[...], lb_ref[...])


def _merge_call(ya, yb, g, x, gate_c, w_a, w_b, ln_g, ln_b, alpha, tm):
    bsz, t, d = x.shape
    half = ya.shape[2]
    rows = lambda n: pl.BlockSpec((1, tm, n), lambda i, j: (i, j, 0))
    full = lambda a: pl.BlockSpec(a.shape, lambda i, j: (0,) * a.ndim)
    return pl.pallas_call(
        functools.partial(_merge_kernel, alpha=alpha),
        name="merge",
        out_shape=jax.ShapeDtypeStruct(x.shape, F32),
        grid=(bsz, t // tm),
        in_specs=[rows(half), rows(half), rows(g.shape[2]), rows(d),
                  pl.BlockSpec((1, 1, d), lambda i, j: (i, 0, 0)),
                  full(w_a), full(w_b), full(ln_g), full(ln_b)],
        out_specs=rows(d),
        compiler_params=_params(("arbitrary", "arbitrary")),
    )(ya, yb, g, x, gate_c, w_a, w_b, ln_g, ln_b)


def _layer_tables(rwkv_mix, w0, w_up, a0, a_up, k_k, k_a, r_k, gn_g, gn_b, width):
    n_pairs = width // LANES
    per_pair = lambda vct: vct.reshape(n_pairs, 1, LANES)
    rows = [rwkv_mix[0:width], rwkv_mix[width:2 * width], rwkv_mix[2 * width:3 * width],
            w0, a0, k_k, k_a, r_k, gn_g, gn_b]
    pp = jnp.concatenate([per_pair(x) for x in rows]
                         + [jnp.zeros((n_pairs, _P_ROWS - len(rows), LANES), F32)], axis=1)
    mixwa = rwkv_mix[3 * width:].reshape(1, 2 * LORA)
    zeros = jnp.zeros((n_pairs, LORA, LANES), F32)
    wu = w_up.reshape(LORA, n_pairs, LANES).transpose(1, 0, 2)
    au = a_up.reshape(LORA, n_pairs, LANES).transpose(1, 0, 2)
    wcomb = jnp.concatenate([jnp.concatenate([wu, zeros], axis=2),
                             jnp.concatenate([zeros, au], axis=2)], axis=1).astype(BF16)
    return pp, mixwa, wcomb


def _fox_tables(heads, tm):
    width = heads * HEAD_DIM
    chan = jnp.arange(width)
    off = chan % LANES
    src = jnp.arange(LANES)[:, None]
    base = lambda hd: (hd // PAIR) * LANES + jnp.where(hd % PAIR == 0, HEAD_DIM, 0)
    at = lambda slot: (chan[None, :] == base(src) + slot) & (src < heads)
    selq = jnp.stack([at(N_SPLIT + i) for i in range(N_SPLIT)]).astype(BF16)
    selk = jnp.stack([at(i).T for i in range(N_SPLIT)]).astype(BF16)
    slot_of = off % HEAD_DIM
    onesq = (slot_of < N_SPLIT).astype(F32).reshape(1, width)
    onesk = jnp.broadcast_to(((slot_of >= N_SPLIT) & (slot_of < 2 * N_SPLIT))
                             .astype(F32)[:, None], (width, tm))
    return selq, onesq, selk, onesk


def kernel(x, c, emb_ln_g, emb_ln_b, w_ada, b_ada, w_in, rwkv_mix, w0, w_up, a0, a_up, k_k, k_a,
           r_k, gn_g, gn_b, fox_bf, w_out, ln_g, ln_b):
    bsz, t, d = x.shape
    depth = w_ada.shape[0]
    width = d // 2
    heads = width // HEAD_DIM
    rw_end = 3 * width + 2 * LORA
    fx_end = rw_end + 3 * width + heads
    alpha = (2 * depth) ** 0.25
    tm = min(TILE, t)

    c8 = jnp.pad(c, ((0, SUBLANES - bsz), (0, 0)))
    mods = _ada_call(c8, w_ada, b_ada)[:, :bsz]
    selq, onesq, selk, onesk = _fox_tables(heads, tm)

    h = _ln_call(x, emb_ln_g, emb_ln_b, tm)
    for l in range(depth):
        shift = mods[l, :, 0:d].reshape(bsz, 1, d)
        scale = mods[l, :, d:2 * d].reshape(bsz, 1, d)
        gate_c = mods[l, :, 2 * d:].reshape(bsz, 1, d)
        w = w_in[l]
        w_rw = w[:, :rw_end].astype(BF16)
        w_q = (w[:, rw_end:rw_end + width] * (HEAD_DIM ** -0.5 * LOG2E)).astype(BF16)
        w_kt = w[:, rw_end + width:rw_end + 2 * width].T.astype(BF16)
        w_v = w[:, rw_end + 2 * width:rw_end + 3 * width].astype(BF16)
        w_f = jnp.pad(w[:, rw_end + 3 * width:fx_end], ((0, 0), (0, LANES - heads))).astype(BF16)
        w_g = w[:, fx_end:].astype(BF16)
        bf = jnp.pad(fox_bf[l], (0, LANES - heads)).reshape(1, LANES)
        rw, q, kt, v, g, qb, kbt = _proj_call(h, scale, shift, w_rw, w_q, w_kt, w_v, w_g, w_f, bf,
                                              selq, onesq, selk, onesk, tm)

        pp, mixwa, wcomb = _layer_tables(rwkv_mix[l], w0[l], w_up[l], a0[l], a_up[l], k_k[l],
                                         k_a[l], r_k[l], gn_g[l], gn_b[l], width)
        ya = _rwkv_call(rw, pp, mixwa, wcomb, width, tm)
        yb = _fox_call(q, qb, kt, kbt, v)
        wo = w_out[l].astype(BF16)
        h = _merge_call(ya, yb, g, h, gate_c, wo[:width], wo[width:], ln_g[l].reshape(1, d),
                        ln_b[l].reshape(1, d), alpha, tm)
    return h
```

```python
import functools

import jax
import jax.numpy as jnp
from jax import lax
from jax.experimental import pallas as pl
from jax.experimental.pallas import tpu as pltpu

F32 = jnp.float32
BF16 = jnp.bfloat16

HEAD_DIM = 64
LANES = 128
SUBLANES = 8
PAIR = LANES // HEAD_DIM
LORA = 64
LN_EPS = 1e-5
GN_EPS = 64e-5
CHUNK = 64
TILE = 512
N_SPLIT = 3
FOX_Q_BLOCKS = 2
FOX_MARGIN = 64.0
LOG2E = 1.4426950408889634
VMEM_LIMIT = 56 * 1024 * 1024


def _params(sem):
    return pltpu.CompilerParams(dimension_semantics=sem, vmem_limit_bytes=VMEM_LIMIT)


def _dot(a, b):
    return jnp.dot(a.astype(BF16), b.astype(BF16), preferred_element_type=F32)


def _dot_nt(a, b):
    return lax.dot_general(a.astype(BF16), b.astype(BF16), (((1,), (1,)), ((), ())),
                           preferred_element_type=F32)


def _split3(x):
    hi = x.astype(BF16)
    r1 = x - hi.astype(F32)
    mid = r1.astype(BF16)
    lo = (r1 - mid.astype(F32)).astype(BF16)
    return hi, mid, lo


def _dot_exact_rhs(a_bf16, x):
    hi, mid, lo = _split3(x)
    f = lambda y: jnp.dot(a_bf16, y, preferred_element_type=F32)
    return f(hi) + f(mid) + f(lo)


def _dot_exact_lhs(x, a_bf16):
    hi, mid, lo = _split3(x)
    f = lambda y: jnp.dot(y, a_bf16, preferred_element_type=F32)
    return f(hi) + f(mid) + f(lo)


def _softplus(z):
    return jnp.maximum(z, 0.0) + jnp.log(1.0 + jnp.exp(-jnp.abs(z)))


def _layer_norm_rows(x, g, b):
    mu = jnp.mean(x, axis=-1, keepdims=True)
    d = x - mu
    var = jnp.mean(d * d, axis=-1, keepdims=True)
    return d * lax.rsqrt(var + LN_EPS) * g + b


def _ada_kernel(c_ref, w_ref, b_ref, o_ref):
    c = c_ref[...]
    w = w_ref[0]
    ch, cm, cl = _split3(c)
    wh, wm, wl = _split3(w)
    f = lambda a, b: jnp.dot(a, b, preferred_element_type=F32)
    acc = f(ch, wh) + (f(ch, wm) + f(cm, wh)) + (f(ch, wl) + f(cm, wm) + f(cl, wh))
    o_ref[0] = acc + b_ref[0]


def _ada_call(c8, w_ada, b_ada):
    depth, d, d3 = w_ada.shape
    return pl.pallas_call(
        _ada_kernel,
        name="ada",
        out_shape=jax.ShapeDtypeStruct((depth, SUBLANES, d3), F32),
        grid=(depth,),
        in_specs=[pl.BlockSpec((SUBLANES, d), lambda l: (0, 0)),
                  pl.BlockSpec((1, d, d3), lambda l: (l, 0, 0)),
                  pl.BlockSpec((1, 1, d3), lambda l: (l, 0, 0))],
        out_specs=pl.BlockSpec((1, SUBLANES, d3), lambda l: (l, 0, 0)),
        compiler_params=_params(("arbitrary",)),
    )(c8, w_ada, b_ada.reshape(depth, 1, d3))


def _ln_kernel(x_ref, g_ref, b_ref, o_ref):
    o_ref[0] = _layer_norm_rows(x_ref[0], g_ref[...], b_ref[...])


def _ln_call(x, g, b, tm):
    bsz, t, d = x.shape
    return pl.pallas_call(
        _ln_kernel,
        name="emb_ln",
        out_shape=jax.ShapeDtypeStruct(x.shape, F32),
        grid=(bsz, t // tm),
        in_specs=[pl.BlockSpec((1, tm, d), lambda i, j: (i, j, 0)),
                  pl.BlockSpec((1, d), lambda i, j: (0, 0)),
                  pl.BlockSpec((1, d), lambda i, j: (0, 0))],
        out_specs=pl.BlockSpec((1, tm, d), lambda i, j: (i, j, 0)),
        compiler_params=_params(("arbitrary", "arbitrary")),
    )(x, g.reshape(1, d), b.reshape(1, d))


def _proj_kernel(x_ref, sc_ref, sh_ref, wrw_ref, wq_ref, wkt_ref, wv_ref, wg_ref, wf_ref, bf_ref,
                 selq_ref, onesq_ref, selk_ref, onesk_ref,
                 rw_ref, q_ref, kt_ref, v_ref, g_ref, qb_ref, kbt_ref, carry_ref):
    tm = x_ref.shape[1]

    @pl.when(pl.program_id(1) == 0)
    def _():
        carry_ref[...] = jnp.zeros_like(carry_ref)

    h = x_ref[0] * (1.0 + sc_ref[0]) + sh_ref[0]
    hb = h.astype(BF16)
    rw_ref[0] = jnp.dot(hb, wrw_ref[...], preferred_element_type=F32)
    g_ref[0] = jnp.dot(hb, wg_ref[...], preferred_element_type=F32)
    q_ref[0] = jnp.dot(hb, wq_ref[...], preferred_element_type=F32).astype(BF16)
    v_ref[0] = jnp.dot(hb, wv_ref[...], preferred_element_type=F32).astype(BF16)
    nt = lambda a, b: lax.dot_general(a, b, (((1,), (1,)), ((), ())), preferred_element_type=F32)
    kt_ref[0, 0] = nt(wkt_ref[...], hb).astype(BF16)

    z = jnp.dot(hb, wf_ref[...], preferred_element_type=F32) + bf_ref[...]
    lf = -_softplus(-z) * LOG2E
    r = lax.broadcasted_iota(jnp.int32, (tm, tm), 0)
    c = lax.broadcasted_iota(jnp.int32, (tm, tm), 1)
    tri = jnp.where(r >= c, 1.0, 0.0).astype(BF16)
    cum = _dot_exact_rhs(tri, lf) + carry_ref[...]
    carry_ref[...] = cum[tm - 1:tm, :]

    pos = _split3(cum)
    neg = _split3(-cum)
    qb = onesq_ref[...]
    kbt = onesk_ref[...]
    for i in range(N_SPLIT):
        qb = qb + jnp.dot(pos[i], selq_ref[i], preferred_element_type=F32)
        kbt = kbt + nt(selk_ref[i], neg[i])
    qb_ref[0] = qb.astype(BF16)
    kbt_ref[0, 0] = kbt.astype(BF16)


def _proj_call(x, scale, shift, w_rw, w_q, w_kt, w_v, w_g, w_f, bf, selq, onesq, selk, onesk, tm):
    bsz, t, d = x.shape
    nt = t // tm
    n_rw, n_g, width = w_rw.shape[1], w_g.shape[1], w_q.shape[1]
    full = lambda a: pl.BlockSpec(a.shape, lambda i, j: (0,) * a.ndim)
    rows = lambda n: pl.BlockSpec((1, tm, n), lambda i, j: (i, j, 0))
    vec = pl.BlockSpec((1, 1, d), lambda i, j: (i, 0, 0))
    tiled = lambda n: pl.BlockSpec((1, 1, n, tm), lambda i, j: (i, j, 0, 0))
    return pl.pallas_call(
        _proj_kernel,
        name="proj",
        out_shape=(jax.ShapeDtypeStruct((bsz, t, n_rw), F32),
                   jax.ShapeDtypeStruct((bsz, t, width), BF16),
                   jax.ShapeDtypeStruct((bsz, nt, width, tm), BF16),
                   jax.ShapeDtypeStruct((bsz, t, width), BF16),
                   jax.ShapeDtypeStruct((bsz, t, n_g), F32),
                   jax.ShapeDtypeStruct((bsz, t, width), BF16),
                   jax.ShapeDtypeStruct((bsz, nt, width, tm), BF16)),
        grid=(bsz, nt),
        in_specs=[rows(d), vec, vec, full(w_rw), full(w_q), full(w_kt), full(w_v), full(w_g),
                  full(w_f), full(bf), full(selq), full(onesq), full(selk), full(onesk)],
        out_specs=(rows(n_rw), rows(width), tiled(width), rows(width), rows(n_g), rows(width),
                   tiled(width)),
        scratch_shapes=[pltpu.VMEM((1, LANES), F32)],
        compiler_params=_params(("arbitrary", "arbitrary")),
    )(x, scale, shift, w_rw, w_q, w_kt, w_v, w_g, w_f, bf, selq, onesq, selk, onesk)


_P_MIX_R, _P_MIX_K, _P_MIX_V, _P_W0, _P_A0, _P_KK, _P_KA, _P_RK, _P_GNG, _P_GNB = range(10)
_P_ROWS = 16


def _rwkv_kernel(r_ref, k_ref, v_ref, wa_ref, pp_ref, mixwa_ref, wcomb_ref, o_ref,
                 carry_ref, state_ref, q1_s, q2_s, g_s, h_s, bonus_s):
    ct = r_ref.shape[1]
    n_chunks = ct // CHUNK
    C = CHUNK
    chunks = range(n_chunks)

    @pl.when(pl.program_id(2) == 0)
    def _():
        for ref in (carry_ref, state_ref, q1_s, q2_s, g_s, h_s, bonus_s):
            ref[...] = jnp.zeros_like(ref)

    pp = pp_ref[0]
    prow = lambda i: pp[i:i + 1, :]
    row_id = lax.broadcasted_iota(jnp.int32, (ct, LANES), 0)
    lane_id = lax.broadcasted_iota(jnp.int32, (ct, LANES), 1)

    def shifted(ref, slot, mix):
        x = ref[0]
        prev = pltpu.roll(x, 1, 0)
        prev = jnp.where(row_id == 0, carry_ref[slot:slot + 1, :], prev)
        carry_ref[slot:slot + 1, :] = x[ct - 1:ct, :]
        return x + (prev - x) * mix

    r = shifted(r_ref, 0, prow(_P_MIX_R))
    k = shifted(k_ref, 1, prow(_P_MIX_K))
    v = shifted(v_ref, 2, prow(_P_MIX_V))
    wa = shifted(wa_ref, 3, mixwa_ref[...])

    t_in = jnp.where(lane_id < LORA, jnp.tanh(wa), wa)
    dwa = _dot(t_in, wcomb_ref[0])
    w_log = -_softplus(-(prow(_P_W0) + dwa[:, :LANES])) - 0.5
    lw = -jnp.exp(w_log)
    a = 1.0 / (1.0 + jnp.exp(-(prow(_P_A0) + dwa[:, LANES:])))

    bd_r = lax.broadcasted_iota(jnp.int32, (LANES, LANES), 0)
    bd_c = lax.broadcasted_iota(jnp.int32, (LANES, LANES), 1)
    same_head = (bd_r // HEAD_DIM) == (bd_c // HEAD_DIM)
    ones_bd = jnp.where(same_head, 1.0, 0.0).astype(BF16)

    kk = k * prow(_P_KK)
    ss = _dot_exact_lhs(kk * kk, ones_bd)
    kap = kk / jnp.maximum(jnp.sqrt(ss), 1e-12)
    k2 = k * (1.0 + (a - 1.0) * prow(_P_KA))
    b = kap * a
    bonus = _dot_exact_lhs(r * k2 * prow(_P_RK), ones_bd) * v

    lane_c = lax.broadcasted_iota(jnp.int32, (C, LANES), 1)
    m1 = jnp.where(lane_c < HEAD_DIM, 1.0, 0.0)
    m2 = 1.0 - m1
    stack = lambda x: jnp.concatenate([x * m1, x * m2], axis=0)
    dup = lambda x: jnp.concatenate([x, x], axis=0)
    sm = jnp.concatenate([m1, m2], axis=0)
    strict_lower = same_head & ((bd_r % C) > (bd_c % C))
    lower = same_head & ((bd_r % C) >= (bd_c % C))
    eye = bd_r == bd_c
    eye_f = jnp.where(eye, 1.0, 0.0)
    tr = lax.broadcasted_iota(jnp.int32, (C, C), 0)
    tc = lax.broadcasted_iota(jnp.int32, (C, C), 1)
    tri = jnp.where(tr >= tc, 1.0, 0.0).astype(BF16)
    gn_g, gn_b = prow(_P_GNG), prow(_P_GNB)
    sl = lambda ci: slice(ci * C, (ci + 1) * C)

    state = [state_ref[...]]

    def emit_previous(ci):
        ys = _dot(q1_s[ci], state[0]) + q2_s[ci]
        state[0] = _dot(g_s[ci], state[0]) + h_s[ci]
        mu = jnp.sum(ys, axis=-1, keepdims=True) * (1.0 / HEAD_DIM)
        d = (ys - mu) * sm
        var = jnp.sum(d * d, axis=-1, keepdims=True) * (1.0 / HEAD_DIM)
        yn = d * lax.rsqrt(var + GN_EPS)
        o_ref[0, sl(ci), :] = (yn[:C] + yn[C:]) * gn_g + gn_b + bonus_s[sl(ci), :]

    pending = list(chunks)

    def hook():
        if pending:
            emit_previous(pending.pop(0))

    L = [_dot_exact_rhs(tri, lw[sl(ci)]) for ci in chunks]
    Lc = [x[C - 1:C, :] for x in L]
    e_neg = [jnp.exp(-x) for x in L]
    e_end = [jnp.exp(xc - x) for xc, x in zip(Lc, L)]
    Ks = [stack(kap[sl(ci)] * jnp.exp(L[ci] - lw[sl(ci)])).astype(BF16) for ci in chunks]
    Rs = [stack(r[sl(ci)] * jnp.exp(L[ci])) for ci in chunks]
    Vs = [stack(v[sl(ci)]).astype(BF16) for ci in chunks]
    K2 = [dup(k2[sl(ci)] * e_neg[ci]).astype(BF16) for ci in chunks]
    B2 = [dup(b[sl(ci)] * e_neg[ci]).astype(BF16) for ci in chunks]
    KendT = [stack(k2[sl(ci)] * e_end[ci]).T.astype(BF16) for ci in chunks]
    BendT = [stack(b[sl(ci)] * e_end[ci]).T.astype(BF16) for ci in chunks]
    N = [jnp.where(strict_lower, _dot_nt(Ks[ci], B2[ci]), 0.0) for ci in chunks]
    Akk = [jnp.where(strict_lower, _dot_nt(Ks[ci], K2[ci]), 0.0).astype(BF16) for ci in chunks]
    hook()
    Ark = [jnp.where(lower, _dot_nt(Rs[ci], K2[ci]), 0.0).astype(BF16) for ci in chunks]
    Arb = [jnp.where(lower, _dot_nt(Rs[ci], B2[ci]), 0.0).astype(BF16) for ci in chunks]
    hook()
    AkkV = [_dot(Akk[ci], Vs[ci]) for ci in chunks]
    ArkV = [_dot(Ark[ci], Vs[ci]) for ci in chunks]
    KV = [_dot(KendT[ci], Vs[ci]) for ci in chunks]
    hook()
    P = [_dot(x, x) for x in N]
    T = [eye_f - x for x in N]
    for _ in range(C.bit_length() - 3):
        both = [_dot(jnp.concatenate([t, p], axis=0), p) for t, p in zip(T, P)]
        T = [t + x[:2 * C] for t, x in zip(T, both)]
        P = [x[2 * C:] for x in both]
        hook()
    Tb = [(t + _dot(t, p)).astype(BF16) for t, p in zip(T, P)]
    hook()
    P1 = [_dot(Tb[ci], Ks[ci]).astype(BF16) for ci in chunks]
    P2 = [_dot(Tb[ci], AkkV[ci]).astype(BF16) for ci in chunks]
    while pending:
        hook()
    state_ref[...] = state[0]
    for ci in chunks:
        q1_s[ci] = Rs[ci] - _dot(Arb[ci], P1[ci])
        q2_s[ci] = ArkV[ci] - _dot(Arb[ci], P2[ci])
        g_s[ci] = (jnp.where(eye, jnp.broadcast_to(jnp.exp(Lc[ci]), (LANES, LANES)), 0.0)
                   - _dot(BendT[ci], P1[ci]))
        h_s[ci] = KV[ci] - _dot(BendT[ci], P2[ci])
    bonus_s[...] = bonus


def _rwkv_call(rw, pp, mixwa, wcomb, width, ct):
    bsz, t, _ = rw.shape
    n_pairs = width // LANES
    nt = t // ct
    last = nt - 1
    blk = lambda off: pl.BlockSpec((1, ct, LANES),
                                   lambda i, p, j, off=off: (i, jnp.minimum(j, last), off + p))
    wa_blk = pl.BlockSpec((1, ct, LANES), lambda i, p, j: (i, jnp.minimum(j, last), 3 * n_pairs))
    staged = pltpu.VMEM((ct // CHUNK, LANES, LANES), F32)
    return pl.pallas_call(
        _rwkv_kernel,
        name="rwkv",
        out_shape=jax.ShapeDtypeStruct((bsz, t, width), F32),
        grid=(bsz, n_pairs, nt + 1),
        in_specs=[blk(0), blk(n_pairs), blk(2 * n_pairs), wa_blk,
                  pl.BlockSpec((1, _P_ROWS, LANES), lambda i, p, j: (p, 0, 0)),
                  pl.BlockSpec((1, LANES), lambda i, p, j: (0, 0)),
                  pl.BlockSpec((1, LANES, 2 * LANES), lambda i, p, j: (p, 0, 0))],
        out_specs=pl.BlockSpec((1, ct, LANES), lambda i, p, j: (i, jnp.maximum(j - 1, 0), p)),
        scratch_shapes=[pltpu.VMEM((SUBLANES, LANES), F32), pltpu.VMEM((LANES, LANES), F32),
                        staged, staged, staged, staged, pltpu.VMEM((ct, LANES), F32)],
        compiler_params=_params(("arbitrary", "arbitrary", "arbitrary")),
    )(rw, rw, rw, rw, pp, mixwa, wcomb)


def _fox_kernel(q_ref, qb_ref, kt_ref, kbt_ref, v_ref, o_ref,
                s_buf, p_buf, m_buf, a_buf, acc_buf, tab_ref):
    bq, bk = q_ref.shape[1], kt_ref.shape[3]
    nt = kt_ref.shape[1]
    qi = pl.program_id(2)
    n_full = qi * FOX_Q_BLOCKS
    heads = range(PAIR)
    lane_q = lax.broadcasted_iota(jnp.int32, (bq, LANES), 1)
    lane_v = lax.broadcasted_iota(jnp.int32, (bk, LANES), 1)
    lane_t = lax.broadcasted_iota(jnp.int32, (1, LANES), 1)
    own_q = (lane_q < HEAD_DIM, lane_q >= HEAD_DIM)
    own_v = (lane_v < HEAD_DIM, lane_v >= HEAD_DIM)
    den_lane = (HEAD_DIM, 0)
    table_at = (HEAD_DIM, 0)
    q, qb = q_ref[0], qb_ref[0]
    q_aug = [jnp.where(own_q[h], q, qb) for h in heads]
    v_ones = [jnp.where(lane_v == den_lane[h], 1.0, 0.0).astype(BF16) for h in heads]
    rr = lax.broadcasted_iota(jnp.int32, (bq, bk), 0)
    cc = lax.broadcasted_iota(jnp.int32, (bq, bk), 1)

    @pl.when(qi == 0)
    def _():
        def one(j, tabs):
            kt = kt_ref[0, j].astype(F32)
            kbt = kbt_ref[0, j].astype(F32)
            sq = kt * kt
            new = []
            for h in heads:
                ssq = jnp.sum(sq[h * HEAD_DIM:(h + 1) * HEAD_DIM], axis=0, keepdims=True)
                new.append(jnp.sqrt(jnp.max(ssq, axis=1, keepdims=True)))
            for h in heads:
                t0 = table_at[h]
                ck = -(kbt[t0:t0 + 1] + kbt[t0 + 1:t0 + 2] + kbt[t0 + 2:t0 + 3])
                new.append(jnp.min(ck, axis=1, keepdims=True))
            return tuple(jnp.where(lane_t == j, x, t) for x, t in zip(new, tabs))

        zero = jnp.zeros((1, LANES), F32)
        tabs = lax.fori_loop(0, nt, one, (zero,) * (2 * PAIR))
        for i, t in enumerate(tabs):
            tab_ref[i:i + 1, :] = t

    bd_r = lax.broadcasted_iota(jnp.int32, (LANES, LANES), 0)
    bd_c = lax.broadcasted_iota(jnp.int32, (LANES, LANES), 1)
    ones_bd = jnp.where((bd_r // HEAD_DIM) == (bd_c // HEAD_DIM), 1.0, 0.0).astype(BF16)
    qf, qbf = q.astype(F32), qb.astype(F32)
    qn_lanes = jnp.sqrt(jnp.max(_dot_exact_lhs(qf * qf, ones_bd), axis=0, keepdims=True))
    qn = [qn_lanes[:, h * HEAD_DIM:h * HEAD_DIM + 1] for h in heads]
    cq_max = []
    for h in heads:
        t0 = table_at[h] + N_SPLIT
        cq = qbf[:, t0:t0 + 1] + qbf[:, t0 + 1:t0 + 2] + qbf[:, t0 + 2:t0 + 3]
        cq_max.append(jnp.max(cq, axis=0, keepdims=True))

    wide = lambda x: jnp.concatenate([x] * (bk // LANES), axis=1)

    def scores(j, slot, m_prev, diag=None):
        kt, kbt = kt_ref[0, j], kbt_ref[0, j]
        k_aug = (jnp.concatenate([kt[:HEAD_DIM], kbt[HEAD_DIM:]], axis=0),
                 jnp.concatenate([kbt[:HEAD_DIM], kt[HEAD_DIM:]], axis=0))
        for h in heads:
            s = jnp.dot(q_aug[h], k_aug[h], preferred_element_type=F32)
            if diag is not None:
                s = jnp.where(cc + diag * bk <= rr, s, -jnp.inf)
            s_buf[slot, h] = s
            bmax = jnp.broadcast_to(jnp.max(s, axis=1, keepdims=True), (bq, LANES))
            m_buf[slot, h] = jnp.maximum(m_prev[h], bmax)

    def softmax(slot, m_prev):
        for h in heads:
            m_cur = m_buf[slot, h]
            a_buf[slot, h] = jnp.exp2(m_prev[h] - m_cur)
            p_buf[slot, h] = jnp.exp2(s_buf[slot, h] - wide(m_cur)).astype(BF16)

    def values(j, slot):
        vb = v_ref[0, pl.ds(pl.multiple_of(j * bk, bk), bk), :]
        for h in heads:
            v_aug = jnp.where(own_v[h], vb, v_ones[h])
            acc_buf[h] = a_buf[slot, h] * acc_buf[h] + jnp.dot(p_buf[slot, h], v_aug,
                                                               preferred_element_type=F32)

    def step(j, slot):
        m_before = [m_buf[1 - slot, h] for h in heads]
        scores(j + 1, 1 - slot, [m_buf[slot, h] for h in heads])
        values(jnp.maximum(j - 1, 0), 1 - slot)
        softmax(slot, m_before)

    def drain(last, slot):
        values(jnp.maximum(last - 1, 0), 1 - slot)
        softmax(slot, [m_buf[1 - slot, h] for h in heads])
        values(last, slot)

    never = [jnp.full((bq, LANES), -jnp.inf, F32)] * PAIR
    acc_buf[...] = jnp.zeros_like(acc_buf)
    scores(n_full, 0, never, diag=0)
    softmax(0, never)
    scores(n_full + 1, 1, [m_buf[0, h] for h in heads], diag=1)
    values(n_full, 0)
    softmax(1, [m_buf[0, h] for h in heads])
    values(n_full + 1, 1)

    needed = lane_t < 0
    for h in heads:
        m_min = jnp.min(m_buf[1, h], axis=0, keepdims=True)
        bound = (qn[h] * tab_ref[h:h + 1, :] * (1.0 + 2.0 ** -8) + 1.0
                 + cq_max[h] - tab_ref[PAIR + h:PAIR + h + 1, :])
        needed = jnp.logical_or(needed, bound > m_min - FOX_MARGIN)
    needed = jnp.logical_and(needed, lane_t < n_full)
    first = jnp.min(jnp.where(needed, lane_t, n_full))
    count = n_full - first

    p_buf[1] = jnp.zeros(p_buf.shape[1:], BF16)
    a_buf[1] = jnp.ones(a_buf.shape[1:], F32)

    @pl.when(count > 0)
    def _():
        scores(first, 0, [m_buf[1, h] for h in heads])

    def pair(jj, carry):
        step(first + 2 * jj, 0)
        step(first + 2 * jj + 1, 1)
        return carry

    lax.fori_loop(0, jnp.maximum(count - 1, 0) // 2, pair, 0)
    even = count % 2 == 0

    @pl.when(jnp.logical_and(count >= 2, even))
    def _():
        step(n_full - 2, 0)
        drain(n_full - 1, 1)

    @pl.when(jnp.logical_not(even))
    def _():
        drain(n_full - 1, 0)

    out = [acc_buf[h] / acc_buf[h][:, den_lane[h]:den_lane[h] + 1] for h in heads]
    o_ref[0] = jnp.where(own_q[0], out[0], out[1])


def _fox_call(q, qb, kt, kbt, v):
    bsz, t, width = q.shape
    nt, bk = kt.shape[1], kt.shape[3]
    bq = FOX_Q_BLOCKS * bk
    n_pairs = width // LANES
    q_blk = pl.BlockSpec((1, bq, LANES), lambda i, p, j: (i, j, p))
    once = pl.Buffered(1)
    kt_blk = pl.BlockSpec((1, nt, LANES, bk), lambda i, p, j: (i, 0, p, 0), pipeline_mode=once)
    v_blk = pl.BlockSpec((1, t, LANES), lambda i, p, j: (i, 0, p), pipeline_mode=once)
    return pl.pallas_call(
        _fox_kernel,
        name="fox",
        out_shape=jax.ShapeDtypeStruct((bsz, t, width), F32),
        grid=(bsz, n_pairs, t // bq),
        in_specs=[q_blk, q_blk, kt_blk, kt_blk, v_blk],
        out_specs=q_blk,
        scratch_shapes=[pltpu.VMEM((2, PAIR, bq, bk), F32), pltpu.VMEM((2, PAIR, bq, bk), BF16),
                        pltpu.VMEM((2, PAIR, bq, LANES), F32), pltpu.VMEM((2, PAIR, bq, LANES), F32),
                        pltpu.VMEM((PAIR, bq, LANES), F32), pltpu.VMEM((SUBLANES, LANES), F32)],
        compiler_params=_params(("arbitrary", "arbitrary", "arbitrary")),
    )(q, qb, kt, kbt, v)


def _merge_kernel(ya_ref, yb_ref, g_ref, x_ref, gc_ref, wa_ref, wb_ref, lg_ref, lb_ref, o_ref,
                  *, alpha):
    half = ya_ref.shape[2]
    g = g_ref[0]
    silu = g / (1.0 + jnp.exp(-g))
    out = (_dot(ya_ref[0] * silu[:, :half], wa_ref[...])
           + _dot(yb_ref[0] * silu[:, half:], wb_ref[...]))
    z = alpha * x_ref[0] + (1.0 + gc_ref[0]) * out
    o_ref[0] = _layer_norm_rows(z, lg_ref[...], lb_ref[...])


def _merge_call(ya, yb, g, x, gate_c, w_a, w_b, ln_g, ln_b, alpha, tm):
    bsz, t, d = x.shape
    half = ya.shape[2]
    rows = lambda n: pl.BlockSpec((1, tm, n), lambda i, j: (i, j, 0))
    full = lambda a: pl.BlockSpec(a.shape, lambda i, j: (0,) * a.ndim)
    return pl.pallas_call(
        functools.partial(_merge_kernel, alpha=alpha),
        name="merge",
        out_shape=jax.ShapeDtypeStruct(x.shape, F32),
        grid=(bsz, t // tm),
        in_specs=[rows(half), rows(half), rows(g.shape[2]), rows(d),
                  pl.BlockSpec((1, 1, d), lambda i, j: (i, 0, 0)),
                  full(w_a), full(w_b), full(ln_g), full(ln_b)],
        out_specs=rows(d),
        compiler_params=_params(("arbitrary", "arbitrary")),
    )(ya, yb, g, x, gate_c, w_a, w_b, ln_g, ln_b)


def _layer_tables(rwkv_mix, w0, w_up, a0, a_up, k_k, k_a, r_k, gn_g, gn_b, width):
    n_pairs = width // LANES
    per_pair = lambda vct: vct.reshape(n_pairs, 1, LANES)
    rows = [rwkv_mix[0:width], rwkv_mix[width:2 * width], rwkv_mix[2 * width:3 * width],
            w0, a0, k_k, k_a, r_k, gn_g, gn_b]
    pp = jnp.concatenate([per_pair(x) for x in rows]
                         + [jnp.zeros((n_pairs, _P_ROWS - len(rows), LANES), F32)], axis=1)
    mixwa = rwkv_mix[3 * width:].reshape(1, 2 * LORA)
    zeros = jnp.zeros((n_pairs, LORA, LANES), F32)
    wu = w_up.reshape(LORA, n_pairs, LANES).transpose(1, 0, 2)
    au = a_up.reshape(LORA, n_pairs, LANES).transpose(1, 0, 2)
    wcomb = jnp.concatenate([jnp.concatenate([wu, zeros], axis=2),
                             jnp.concatenate([zeros, au], axis=2)], axis=1).astype(BF16)
    return pp, mixwa, wcomb


def _fox_tables(heads, tm):
    width = heads * HEAD_DIM
    chan = jnp.arange(width)
    off = chan % LANES
    src = jnp.arange(LANES)[:, None]
    base = lambda hd: (hd // PAIR) * LANES + jnp.where(hd % PAIR == 0, HEAD_DIM, 0)
    at = lambda slot: (chan[None, :] == base(src) + slot) & (src < heads)
    selq = jnp.stack([at(N_SPLIT + i) for i in range(N_SPLIT)]).astype(BF16)
    selk = jnp.stack([at(i).T for i in range(N_SPLIT)]).astype(BF16)
    slot_of = off % HEAD_DIM
    onesq = (slot_of < N_SPLIT).astype(F32).reshape(1, width)
    onesk = jnp.broadcast_to(((slot_of >= N_SPLIT) & (slot_of < 2 * N_SPLIT))
                             .astype(F32)[:, None], (width, tm))
    return selq, onesq, selk, onesk


def kernel(x, c, emb_ln_g, emb_ln_b, w_ada, b_ada, w_in, rwkv_mix, w0, w_up, a0, a_up, k_k, k_a,
           r_k, gn_g, gn_b, fox_bf, w_out, ln_g, ln_b):
    bsz, t, d = x.shape
    depth = w_ada.shape[0]
    width = d // 2
    heads = width // HEAD_DIM
    rw_end = 3 * width + 2 * LORA
    fx_end = rw_end + 3 * width + heads
    alpha = (2 * depth) ** 0.25
    tm = min(TILE, t)

    c8 = jnp.pad(c, ((0, SUBLANES - bsz), (0, 0)))
    mods = _ada_call(c8, w_ada, b_ada)[:, :bsz]
    selq, onesq, selk, onesk = _fox_tables(heads, tm)

    h = _ln_call(x, emb_ln_g, emb_ln_b, tm)
    for l in range(depth):
        shift = mods[l, :, 0:d].reshape(bsz, 1, d)
        scale = mods[l, :, d:2 * d].reshape(bsz, 1, d)
        gate_c = mods[l, :, 2 * d:].reshape(bsz, 1, d)
        w = w_in[l]
        w_rw = w[:, :rw_end].astype(BF16)
        w_q = (w[:, rw_end:rw_end + width] * (HEAD_DIM ** -0.5 * LOG2E)).astype(BF16)
        w_kt = w[:, rw_end + width:rw_end + 2 * width].T.astype(BF16)
        w_v = w[:, rw_end + 2 * width:rw_end + 3 * width].astype(BF16)
        w_f = jnp.pad(w[:, rw_end + 3 * width:fx_end], ((0, 0), (0, LANES - heads))).astype(BF16)
        w_g = w[:, fx_end:].astype(BF16)
        bf = jnp.pad(fox_bf[l], (0, LANES - heads)).reshape(1, LANES)
        rw, q, kt, v, g, qb, kbt = _proj_call(h, scale, shift, w_rw, w_q, w_kt, w_v, w_g, w_f, bf,
                                              selq, onesq, selk, onesk, tm)

        pp, mixwa, wcomb = _layer_tables(rwkv_mix[l], w0[l], w_up[l], a0[l], a_up[l], k_k[l],
                                         k_a[l], r_k[l], gn_g[l], gn_b[l], width)
        ya = _rwkv_call(rw, pp, mixwa, wcomb, width, tm)
        yb = _fox_call(q, qb, kt, kbt, v)
        wo = w_out[l].astype(BF16)
        h = _merge_call(ya, yb, g, h, gate_c, wo[:width], wo[width:], ln_g[l].reshape(1, d),
                        ln_b[l].reshape(1, d), alpha, tm)
    return h
```

```python
import functools

import jax
import jax.numpy as jnp
from jax import lax
from jax.experimental import pallas as pl
from jax.experimental.pallas import tpu as pltpu

F32 = jnp.float32
BF16 = jnp.bfloat16

HEAD_DIM = 64
LANES = 128
SUBLANES = 8
PAIR = LANES // HEAD_DIM
LORA = 64
LN_EPS = 1e-5
GN_EPS = 64e-5
CHUNK = 64
TILE = 512
N_SPLIT = 3
FOX_Q_BLOCKS = 2
FOX_MARGIN = 64.0
LOG2E = 1.4426950408889634
VMEM_LIMIT = 56 * 1024 * 1024


def _params(sem):
    return pltpu.CompilerParams(dimension_semantics=sem, vmem_limit_bytes=VMEM_LIMIT)


def _dot(a, b):
    return jnp.dot(a.astype(BF16), b.astype(BF16), preferred_element_type=F32)


def _dot_nt(a, b):
    return lax.dot_general(a.astype(BF16), b.astype(BF16), (((1,), (1,)), ((), ())),
                           preferred_element_type=F32)


def _split3(x):
    hi = x.astype(BF16)
    r1 = x - hi.astype(F32)
    mid = r1.astype(BF16)
    lo = (r1 - mid.astype(F32)).astype(BF16)
    return hi, mid, lo


def _dot_exact_rhs(a_bf16, x):
    hi, mid, lo = _split3(x)
    f = lambda y: jnp.dot(a_bf16, y, preferred_element_type=F32)
    return f(hi) + f(mid) + f(lo)


def _dot_exact_lhs(x, a_bf16):
    hi, mid, lo = _split3(x)
    f = lambda y: jnp.dot(y, a_bf16, preferred_element_type=F32)
    return f(hi) + f(mid) + f(lo)


def _softplus(z):
    return jnp.maximum(z, 0.0) + jnp.log(1.0 + jnp.exp(-jnp.abs(z)))


def _layer_norm_rows(x, g, b):
    mu = jnp.mean(x, axis=-1, keepdims=True)
    d = x - mu
    var = jnp.mean(d * d, axis=-1, keepdims=True)
    return d * lax.rsqrt(var + LN_EPS) * g + b


def _ada_kernel(c_ref, w_ref, b_ref, o_ref):
    c = c_ref[...]
    w = w_ref[0]
    ch, cm, cl = _split3(c)
    wh, wm, wl = _split3(w)
    f = lambda a, b: jnp.dot(a, b, preferred_element_type=F32)
    acc = f(ch, wh) + (f(ch, wm) + f(cm, wh)) + (f(ch, wl) + f(cm, wm) + f(cl, wh))
    o_ref[0] = acc + b_ref[0]


def _ada_call(c8, w_ada, b_ada):
    depth, d, d3 = w_ada.shape
    return pl.pallas_call(
        _ada_kernel,
        name="ada",
        out_shape=jax.ShapeDtypeStruct((depth, SUBLANES, d3), F32),
        grid=(depth,),
        in_specs=[pl.BlockSpec((SUBLANES, d), lambda l: (0, 0)),
                  pl.BlockSpec((1, d, d3), lambda l: (l, 0, 0)),
                  pl.BlockSpec((1, 1, d3), lambda l: (l, 0, 0))],
        out_specs=pl.BlockSpec((1, SUBLANES, d3), lambda l: (l, 0, 0)),
        compiler_params=_params(("arbitrary",)),
    )(c8, w_ada, b_ada.reshape(depth, 1, d3))


def _proj_kernel(x_ref, sc_ref, sh_ref, lng_ref, lnb_ref, wrw_ref, wq_ref, wkt_ref, wv_ref, wg_ref,
                 wf_ref, bf_ref, selq_ref, onesq_ref, selk_ref, onesk_ref,
                 rw_ref, q_ref, kt_ref, v_ref, g_ref, qb_ref, kbt_ref, *rest, normalize):
    tm = x_ref.shape[1]
    carry_ref = rest[-1]

    @pl.when(pl.program_id(1) == 0)
    def _():
        carry_ref[...] = jnp.zeros_like(carry_ref)

    x = x_ref[0]
    if normalize:
        x = _layer_norm_rows(x, lng_ref[...], lnb_ref[...])
        rest[0][0] = x
    h = x * (1.0 + sc_ref[0]) + sh_ref[0]
    hb = h.astype(BF16)
    rw_ref[0] = jnp.dot(hb, wrw_ref[...], preferred_element_type=F32)
    g_ref[0] = jnp.dot(hb, wg_ref[...], preferred_element_type=F32).astype(BF16)
    q_ref[0] = jnp.dot(hb, wq_ref[...], preferred_element_type=F32).astype(BF16)
    v_ref[0] = jnp.dot(hb, wv_ref[...], preferred_element_type=F32).astype(BF16)
    nt = lambda a, b: lax.dot_general(a, b, (((1,), (1,)), ((), ())), preferred_element_type=F32)
    kt_ref[0, 0] = nt(wkt_ref[...], hb).astype(BF16)

    z = jnp.dot(hb, wf_ref[...], preferred_element_type=F32) + bf_ref[...]
    lf = -_softplus(-z) * LOG2E
    r = lax.broadcasted_iota(jnp.int32, (tm, tm), 0)
    c = lax.broadcasted_iota(jnp.int32, (tm, tm), 1)
    tri = jnp.where(r >= c, 1.0, 0.0).astype(BF16)
    cum = _dot_exact_rhs(tri, lf) + carry_ref[...]
    carry_ref[...] = cum[tm - 1:tm, :]

    pos = _split3(cum)
    neg = _split3(-cum)
    qb = onesq_ref[...]
    kbt = onesk_ref[...]
    for i in range(N_SPLIT):
        qb = qb + jnp.dot(pos[i], selq_ref[i], preferred_element_type=F32)
        kbt = kbt + nt(selk_ref[i], neg[i])
    qb_ref[0] = qb.astype(BF16)
    kbt_ref[0, 0] = kbt.astype(BF16)


def _proj_call(x, scale, shift, ln_g, ln_b, w_rw, w_q, w_kt, w_v, w_g, w_f, bf, selq, onesq, selk,
               onesk, tm, normalize):
    bsz, t, d = x.shape
    nt = t // tm
    n_rw, n_g, width = w_rw.shape[1], w_g.shape[1], w_q.shape[1]
    full = lambda a: pl.BlockSpec(a.shape, lambda i, j: (0,) * a.ndim)
    rows = lambda n: pl.BlockSpec((1, tm, n), lambda i, j: (i, j, 0))
    vec = pl.BlockSpec((1, 1, d), lambda i, j: (i, 0, 0))
    tiled = lambda n: pl.BlockSpec((1, 1, n, tm), lambda i, j: (i, j, 0, 0))
    out_shape = [jax.ShapeDtypeStruct((bsz, t, n_rw), F32),
                 jax.ShapeDtypeStruct((bsz, t, width), BF16),
                 jax.ShapeDtypeStruct((bsz, nt, width, tm), BF16),
                 jax.ShapeDtypeStruct((bsz, t, width), BF16),
                 jax.ShapeDtypeStruct((bsz, t, n_g), BF16),
                 jax.ShapeDtypeStruct((bsz, t, width), BF16),
                 jax.ShapeDtypeStruct((bsz, nt, width, tm), BF16)]
    out_specs = [rows(n_rw), rows(width), tiled(width), rows(width), rows(n_g), rows(width),
                 tiled(width)]
    if normalize:
        out_shape.append(jax.ShapeDtypeStruct(x.shape, F32))
        out_specs.append(rows(d))
    return pl.pallas_call(
        functools.partial(_proj_kernel, normalize=normalize),
        name="proj_ln" if normalize else "proj",
        out_shape=tuple(out_shape),
        grid=(bsz, nt),
        in_specs=[rows(d), vec, vec, full(ln_g), full(ln_b), full(w_rw), full(w_q), full(w_kt),
                  full(w_v), full(w_g), full(w_f), full(bf), full(selq), full(onesq), full(selk),
                  full(onesk)],
        out_specs=tuple(out_specs),
        scratch_shapes=[pltpu.VMEM((1, LANES), F32)],
        compiler_params=_params(("arbitrary", "arbitrary")),
    )(x, scale, shift, ln_g, ln_b, w_rw, w_q, w_kt, w_v, w_g, w_f, bf, selq, onesq, selk, onesk)


_P_MIX_R, _P_MIX_K, _P_MIX_V, _P_W0, _P_A0, _P_KK, _P_KA, _P_RK, _P_GNG, _P_GNB = range(10)
_P_ROWS = 16


def _rwkv_kernel(r_ref, k_ref, v_ref, wa_ref, pp_ref, mixwa_ref, wcomb_ref, o_ref,
                 carry_ref, state_ref, q1_s, q2_s, g_s, h_s, bonus_s):
    ct = r_ref.shape[1]
    n_chunks = ct // CHUNK
    C = CHUNK
    chunks = range(n_chunks)

    @pl.when(pl.program_id(2) == 0)
    def _():
        for ref in (carry_ref, state_ref, q1_s, q2_s, g_s, h_s, bonus_s):
            ref[...] = jnp.zeros_like(ref)

    pp = pp_ref[0]
    prow = lambda i: pp[i:i + 1, :]
    row_id = lax.broadcasted_iota(jnp.int32, (ct, LANES), 0)
    lane_id = lax.broadcasted_iota(jnp.int32, (ct, LANES), 1)

    def shifted(ref, slot, mix):
        x = ref[0]
        prev = pltpu.roll(x, 1, 0)
        prev = jnp.where(row_id == 0, carry_ref[slot:slot + 1, :], prev)
        carry_ref[slot:slot + 1, :] = x[ct - 1:ct, :]
        return x + (prev - x) * mix

    r = shifted(r_ref, 0, prow(_P_MIX_R))
    k = shifted(k_ref, 1, prow(_P_MIX_K))
    v = shifted(v_ref, 2, prow(_P_MIX_V))
    wa = shifted(wa_ref, 3, mixwa_ref[...])

    t_in = jnp.where(lane_id < LORA, jnp.tanh(wa), wa)
    dwa = _dot(t_in, wcomb_ref[0])
    w_log = -_softplus(-(prow(_P_W0) + dwa[:, :LANES])) - 0.5
    lw = -jnp.exp(w_log)
    a = 1.0 / (1.0 + jnp.exp(-(prow(_P_A0) + dwa[:, LANES:])))

    bd_r = lax.broadcasted_iota(jnp.int32, (LANES, LANES), 0)
    bd_c = lax.broadcasted_iota(jnp.int32, (LANES, LANES), 1)
    same_head = (bd_r // HEAD_DIM) == (bd_c // HEAD_DIM)
    ones_bd = jnp.where(same_head, 1.0, 0.0).astype(BF16)

    kk = k * prow(_P_KK)
    ss = _dot_exact_lhs(kk * kk, ones_bd)
    kap = kk / jnp.maximum(jnp.sqrt(ss), 1e-12)
    k2 = k * (1.0 + (a - 1.0) * prow(_P_KA))
    b = kap * a
    bonus = _dot_exact_lhs(r * k2 * prow(_P_RK), ones_bd) * v

    lane_c = lax.broadcasted_iota(jnp.int32, (C, LANES), 1)
    m1 = jnp.where(lane_c < HEAD_DIM, 1.0, 0.0)
    m2 = 1.0 - m1
    stack = lambda x: jnp.concatenate([x * m1, x * m2], axis=0)
    dup = lambda x: jnp.concatenate([x, x], axis=0)
    sm = jnp.concatenate([m1, m2], axis=0)
    strict_lower = same_head & ((bd_r % C) > (bd_c % C))
    lower = same_head & ((bd_r % C) >= (bd_c % C))
    eye = bd_r == bd_c
    eye_f = jnp.where(eye, 1.0, 0.0)
    tr = lax.broadcasted_iota(jnp.int32, (C, C), 0)
    tc = lax.broadcasted_iota(jnp.int32, (C, C), 1)
    tri = jnp.where(tr >= tc, 1.0, 0.0).astype(BF16)
    gn_g, gn_b = prow(_P_GNG), prow(_P_GNB)
    sl = lambda ci: slice(ci * C, (ci + 1) * C)

    state = [state_ref[...]]

    def emit_previous(ci):
        ys = _dot(q1_s[ci], state[0]) + q2_s[ci]
        state[0] = _dot(g_s[ci], state[0]) + h_s[ci]
        mu = jnp.sum(ys, axis=-1, keepdims=True) * (1.0 / HEAD_DIM)
        d = (ys - mu) * sm
        var = jnp.sum(d * d, axis=-1, keepdims=True) * (1.0 / HEAD_DIM)
        yn = d * lax.rsqrt(var + GN_EPS)
        o_ref[0, sl(ci), :] = ((yn[:C] + yn[C:]) * gn_g + gn_b
                               + bonus_s[sl(ci), :]).astype(o_ref.dtype)

    pending = list(chunks)

    def hook():
        if pending:
            emit_previous(pending.pop(0))

    L = [_dot_exact_rhs(tri, lw[sl(ci)]) for ci in chunks]
    Lc = [x[C - 1:C, :] for x in L]
    e_neg = [jnp.exp(-x) for x in L]
    e_end = [jnp.exp(xc - x) for xc, x in zip(Lc, L)]
    Ks = [stack(kap[sl(ci)] * jnp.exp(L[ci] - lw[sl(ci)])).astype(BF16) for ci in chunks]
    Rs = [stack(r[sl(ci)] * jnp.exp(L[ci])) for ci in chunks]
    Vs = [stack(v[sl(ci)]).astype(BF16) for ci in chunks]
    K2 = [dup(k2[sl(ci)] * e_neg[ci]).astype(BF16) for ci in chunks]
    B2 = [dup(b[sl(ci)] * e_neg[ci]).astype(BF16) for ci in chunks]
    KendT = [stack(k2[sl(ci)] * e_end[ci]).T.astype(BF16) for ci in chunks]
    BendT = [stack(b[sl(ci)] * e_end[ci]).T.astype(BF16) for ci in chunks]
    N = [jnp.where(strict_lower, _dot_nt(Ks[ci], B2[ci]), 0.0) for ci in chunks]
    Akk = [jnp.where(strict_lower, _dot_nt(Ks[ci], K2[ci]), 0.0).astype(BF16) for ci in chunks]
    hook()
    Ark = [jnp.where(lower, _dot_nt(Rs[ci], K2[ci]), 0.0).astype(BF16) for ci in chunks]
    Arb = [jnp.where(lower, _dot_nt(Rs[ci], B2[ci]), 0.0).astype(BF16) for ci in chunks]
    hook()
    AkkV = [_dot(Akk[ci], Vs[ci]) for ci in chunks]
    ArkV = [_dot(Ark[ci], Vs[ci]) for ci in chunks]
    KV = [_dot(KendT[ci], Vs[ci]) for ci in chunks]
    hook()
    P = [_dot(x, x) for x in N]
    T = [eye_f - x for x in N]
    for _ in range(C.bit_length() - 3):
        both = [_dot(jnp.concatenate([t, p], axis=0), p) for t, p in zip(T, P)]
        T = [t + x[:2 * C] for t, x in zip(T, both)]
        P = [x[2 * C:] for x in both]
        hook()
    Tb = [(t + _dot(t, p)).astype(BF16) for t, p in zip(T, P)]
    hook()
    P1 = [_dot(Tb[ci], Ks[ci]).astype(BF16) for ci in chunks]
    P2 = [_dot(Tb[ci], AkkV[ci]).astype(BF16) for ci in chunks]
    while pending:
        hook()
    state_ref[...] = state[0]
    for ci in chunks:
        q1_s[ci] = Rs[ci] - _dot(Arb[ci], P1[ci])
        q2_s[ci] = ArkV[ci] - _dot(Arb[ci], P2[ci])
        g_s[ci] = (jnp.where(eye, jnp.broadcast_to(jnp.exp(Lc[ci]), (LANES, LANES)), 0.0)
                   - _dot(BendT[ci], P1[ci]))
        h_s[ci] = KV[ci] - _dot(BendT[ci], P2[ci])
    bonus_s[...] = bonus


def _rwkv_call(rw, pp, mixwa, wcomb, width, ct):
    bsz, t, _ = rw.shape
    n_pairs = width // LANES
    nt = t // ct
    last = nt - 1
    blk = lambda off: pl.BlockSpec((1, ct, LANES),
                                   lambda i, p, j, off=off: (i, jnp.minimum(j, last), off + p))
    wa_blk = pl.BlockSpec((1, ct, LANES), lambda i, p, j: (i, jnp.minimum(j, last), 3 * n_pairs))
    staged = pltpu.VMEM((ct // CHUNK, LANES, LANES), F32)
    return pl.pallas_call(
        _rwkv_kernel,
        name="rwkv",
        out_shape=jax.ShapeDtypeStruct((bsz, t, width), BF16),
        grid=(bsz, n_pairs, nt + 1),
        in_specs=[blk(0), blk(n_pairs), blk(2 * n_pairs), wa_blk,
                  pl.BlockSpec((1, _P_ROWS, LANES), lambda i, p, j: (p, 0, 0)),
                  pl.BlockSpec((1, LANES), lambda i, p, j: (0, 0)),
                  pl.BlockSpec((1, LANES, 2 * LANES), lambda i, p, j: (p, 0, 0))],
        out_specs=pl.BlockSpec((1, ct, LANES), lambda i, p, j: (i, jnp.maximum(j - 1, 0), p)),
        scratch_shapes=[pltpu.VMEM((SUBLANES, LANES), F32), pltpu.VMEM((LANES, LANES), F32),
                        staged, staged, staged, staged, pltpu.VMEM((ct, LANES), F32)],
        compiler_params=_params(("arbitrary", "arbitrary", "arbitrary")),
    )(rw, rw, rw, rw, pp, mixwa, wcomb)


def _fox_kernel(q_ref, qb_ref, kt_ref, kbt_ref, v_ref, o_ref,
                s_buf, p_buf, m_buf, a_buf, acc_buf, tab_ref):
    bq, bk = q_ref.shape[1], kt_ref.shape[3]
    nt = kt_ref.shape[1]
    qi = pl.program_id(2)
    n_full = qi * FOX_Q_BLOCKS
    heads = range(PAIR)
    lane_q = lax.broadcasted_iota(jnp.int32, (bq, LANES), 1)
    lane_v = lax.broadcasted_iota(jnp.int32, (bk, LANES), 1)
    lane_t = lax.broadcasted_iota(jnp.int32, (1, LANES), 1)
    own_q = (lane_q < HEAD_DIM, lane_q >= HEAD_DIM)
    own_v = (lane_v < HEAD_DIM, lane_v >= HEAD_DIM)
    den_lane = (HEAD_DIM, 0)
    table_at = (HEAD_DIM, 0)
    q, qb = q_ref[0], qb_ref[0]
    q_aug = [jnp.where(own_q[h], q, qb) for h in heads]
    v_ones = [jnp.where(lane_v == den_lane[h], 1.0, 0.0).astype(BF16) for h in heads]
    rr = lax.broadcasted_iota(jnp.int32, (bq, bk), 0)
    cc = lax.broadcasted_iota(jnp.int32, (bq, bk), 1)

    @pl.when(qi == 0)
    def _():
        def one(j, tabs):
            kt = kt_ref[0, j].astype(F32)
            kbt = kbt_ref[0, j].astype(F32)
            sq = kt * kt
            new = []
            for h in heads:
                ssq = jnp.sum(sq[h * HEAD_DIM:(h + 1) * HEAD_DIM], axis=0, keepdims=True)
                new.append(jnp.sqrt(jnp.max(ssq, axis=1, keepdims=True)))
            for h in heads:
                t0 = table_at[h]
                ck = -(kbt[t0:t0 + 1] + kbt[t0 + 1:t0 + 2] + kbt[t0 + 2:t0 + 3])
                new.append(jnp.min(ck, axis=1, keepdims=True))
            return tuple(jnp.where(lane_t == j, x, t) for x, t in zip(new, tabs))

        zero = jnp.zeros((1, LANES), F32)
        tabs = lax.fori_loop(0, nt, one, (zero,) * (2 * PAIR))
        for i, t in enumerate(tabs):
            tab_ref[i:i + 1, :] = t

    bd_r = lax.broadcasted_iota(jnp.int32, (LANES, LANES), 0)
    bd_c = lax.broadcasted_iota(jnp.int32, (LANES, LANES), 1)
    ones_bd = jnp.where((bd_r // HEAD_DIM) == (bd_c // HEAD_DIM), 1.0, 0.0).astype(BF16)
    qf, qbf = q.astype(F32), qb.astype(F32)
    qn_lanes = jnp.sqrt(jnp.max(_dot_exact_lhs(qf * qf, ones_bd), axis=0, keepdims=True))
    qn = [qn_lanes[:, h * HEAD_DIM:h * HEAD_DIM + 1] for h in heads]
    cq_max = []
    for h in heads:
        t0 = table_at[h] + N_SPLIT
        cq = qbf[:, t0:t0 + 1] + qbf[:, t0 + 1:t0 + 2] + qbf[:, t0 + 2:t0 + 3]
        cq_max.append(jnp.max(cq, axis=0, keepdims=True))

    wide = lambda x: jnp.concatenate([x] * (bk // LANES), axis=1)

    def scores(j, slot, m_prev, diag=None):
        kt, kbt = kt_ref[0, j], kbt_ref[0, j]
        k_aug = (jnp.concatenate([kt[:HEAD_DIM], kbt[HEAD_DIM:]], axis=0),
                 jnp.concatenate([kbt[:HEAD_DIM], kt[HEAD_DIM:]], axis=0))
        for h in heads:
            s = jnp.dot(q_aug[h], k_aug[h], preferred_element_type=F32)
            if diag is not None:
                s = jnp.where(cc + diag * bk <= rr, s, -jnp.inf)
            s_buf[slot, h] = s
            bmax = jnp.broadcast_to(jnp.max(s, axis=1, keepdims=True), (bq, LANES))
            m_buf[slot, h] = jnp.maximum(m_prev[h], bmax)

    def softmax(slot, m_prev):
        for h in heads:
            m_cur = m_buf[slot, h]
            a_buf[slot, h] = jnp.exp2(m_prev[h] - m_cur)
            p_buf[slot, h] = jnp.exp2(s_buf[slot, h] - wide(m_cur)).astype(BF16)

    def values(j, slot):
        vb = v_ref[0, pl.ds(pl.multiple_of(j * bk, bk), bk), :]
        for h in heads:
            v_aug = jnp.where(own_v[h], vb, v_ones[h])
            acc_buf[h] = a_buf[slot, h] * acc_buf[h] + jnp.dot(p_buf[slot, h], v_aug,
                                                               preferred_element_type=F32)

    def step(j, slot):
        m_before = [m_buf[1 - slot, h] for h in heads]
        scores(j + 1, 1 - slot, [m_buf[slot, h] for h in heads])
        values(jnp.maximum(j - 1, 0), 1 - slot)
        softmax(slot, m_before)

    def drain(last, slot):
        values(jnp.maximum(last - 1, 0), 1 - slot)
        softmax(slot, [m_buf[1 - slot, h] for h in heads])
        values(last, slot)

    never = [jnp.full((bq, LANES), -jnp.inf, F32)] * PAIR
    acc_buf[...] = jnp.zeros_like(acc_buf)
    scores(n_full, 0, never, diag=0)
    softmax(0, never)
    scores(n_full + 1, 1, [m_buf[0, h] for h in heads], diag=1)
    values(n_full, 0)
    softmax(1, [m_buf[0, h] for h in heads])
    values(n_full + 1, 1)

    needed = lane_t < 0
    for h in heads:
        m_min = jnp.min(m_buf[1, h], axis=0, keepdims=True)
        bound = (qn[h] * tab_ref[h:h + 1, :] * (1.0 + 2.0 ** -8) + 1.0
                 + cq_max[h] - tab_ref[PAIR + h:PAIR + h + 1, :])
        needed = jnp.logical_or(needed, bound > m_min - FOX_MARGIN)
    needed = jnp.logical_and(needed, lane_t < n_full)
    first = jnp.min(jnp.where(needed, lane_t, n_full))
    count = n_full - first

    p_buf[1] = jnp.zeros(p_buf.shape[1:], BF16)
    a_buf[1] = jnp.ones(a_buf.shape[1:], F32)

    @pl.when(count > 0)
    def _():
        scores(first, 0, [m_buf[1, h] for h in heads])

    def pair(jj, carry):
        step(first + 2 * jj, 0)
        step(first + 2 * jj + 1, 1)
        return carry

    lax.fori_loop(0, jnp.maximum(count - 1, 0) // 2, pair, 0)
    even = count % 2 == 0

    @pl.when(jnp.logical_and(count >= 2, even))
    def _():
        step(n_full - 2, 0)
        drain(n_full - 1, 1)

    @pl.when(jnp.logical_not(even))
    def _():
        drain(n_full - 1, 0)

    out = [acc_buf[h] / acc_buf[h][:, den_lane[h]:den_lane[h] + 1] for h in heads]
    o_ref[0] = jnp.where(own_q[0], out[0], out[1]).astype(o_ref.dtype)


def _fox_call(q, qb, kt, kbt, v):
    bsz, t, width = q.shape
    nt, bk = kt.shape[1], kt.shape[3]
    bq = FOX_Q_BLOCKS * bk
    n_pairs = width // LANES
    q_blk = pl.BlockSpec((1, bq, LANES), lambda i, p, j: (i, j, p))
    once = pl.Buffered(1)
    kt_blk = pl.BlockSpec((1, nt, LANES, bk), lambda i, p, j: (i, 0, p, 0), pipeline_mode=once)
    v_blk = pl.BlockSpec((1, t, LANES), lambda i, p, j: (i, 0, p), pipeline_mode=once)
    return pl.pallas_call(
        _fox_kernel,
        name="fox",
        out_shape=jax.ShapeDtypeStruct((bsz, t, width), BF16),
        grid=(bsz, n_pairs, t // bq),
        in_specs=[q_blk, q_blk, kt_blk, kt_blk, v_blk],
        out_specs=q_blk,
        scratch_shapes=[pltpu.VMEM((2, PAIR, bq, bk), F32), pltpu.VMEM((2, PAIR, bq, bk), BF16),
                        pltpu.VMEM((2, PAIR, bq, LANES), F32), pltpu.VMEM((2, PAIR, bq, LANES), F32),
                        pltpu.VMEM((PAIR, bq, LANES), F32), pltpu.VMEM((SUBLANES, LANES), F32)],
        compiler_params=_params(("arbitrary", "arbitrary", "arbitrary")),
    )(q, qb, kt, kbt, v)


def _merge_kernel(ya_ref, yb_ref, g_ref, x_ref, gc_ref, wa_ref, wb_ref, lg_ref, lb_ref, o_ref,
                  *, alpha):
    half = ya_ref.shape[2]
    g = g_ref[0].astype(F32)
    silu = g / (1.0 + jnp.exp(-g))
    out = (_dot(ya_ref[0].astype(F32) * silu[:, :half], wa_ref[...])
           + _dot(yb_ref[0].astype(F32) * silu[:, half:], wb_ref[...]))
    z = alpha * x_ref[0] + (1.0 + gc_ref[0]) * out
    o_ref[0] = _layer_norm_rows(z, lg_ref[...], lb_ref[...])


def _merge_call(ya, yb, g, x, gate_c, w_a, w_b, ln_g, ln_b, alpha, tm):
    bsz, t, d = x.shape
    half = ya.shape[2]
    rows = lambda n: pl.BlockSpec((1, tm, n), lambda i, j: (i, j, 0))
    full = lambda a: pl.BlockSpec(a.shape, lambda i, j: (0,) * a.ndim)
    return pl.pallas_call(
        functools.partial(_merge_kernel, alpha=alpha),
        name="merge",
        out_shape=jax.ShapeDtypeStruct(x.shape, F32),
        grid=(bsz, t // tm),
        in_specs=[rows(half), rows(half), rows(g.shape[2]), rows(d),
                  pl.BlockSpec((1, 1, d), lambda i, j: (i, 0, 0)),
                  full(w_a), full(w_b), full(ln_g), full(ln_b)],
        out_specs=rows(d),
        compiler_params=_params(("arbitrary", "arbitrary")),
    )(ya, yb, g, x, gate_c, w_a, w_b, ln_g, ln_b)


def _layer_tables(rwkv_mix, w0, w_up, a0, a_up, k_k, k_a, r_k, gn_g, gn_b, width):
    n_pairs = width // LANES
    per_pair = lambda vct: vct.reshape(n_pairs, 1, LANES)
    rows = [rwkv_mix[0:width], rwkv_mix[width:2 * width], rwkv_mix[2 * width:3 * width],
            w0, a0, k_k, k_a, r_k, gn_g, gn_b]
    pp = jnp.concatenate([per_pair(x) for x in rows]
                         + [jnp.zeros((n_pairs, _P_ROWS - len(rows), LANES), F32)], axis=1)
    mixwa = rwkv_mix[3 * width:].reshape(1, 2 * LORA)
    zeros = jnp.zeros((n_pairs, LORA, LANES), F32)
    wu = w_up.reshape(LORA, n_pairs, LANES).transpose(1, 0, 2)
    au = a_up.reshape(LORA, n_pairs, LANES).transpose(1, 0, 2)
    wcomb = jnp.concatenate([jnp.concatenate([wu, zeros], axis=2),
                             jnp.concatenate([zeros, au], axis=2)], axis=1).astype(BF16)
    return pp, mixwa, wcomb


def _fox_tables(heads, tm):
    width = heads * HEAD_DIM
    chan = jnp.arange(width)
    off = chan % LANES
    src = jnp.arange(LANES)[:, None]
    base = lambda hd: (hd // PAIR) * LANES + jnp.where(hd % PAIR == 0, HEAD_DIM, 0)
    at = lambda slot: (chan[None, :] == base(src) + slot) & (src < heads)
    selq = jnp.stack([at(N_SPLIT + i) for i in range(N_SPLIT)]).astype(BF16)
    selk = jnp.stack([at(i).T for i in range(N_SPLIT)]).astype(BF16)
    slot_of = off % HEAD_DIM
    onesq = (slot_of < N_SPLIT).astype(F32).reshape(1, width)
    onesk = jnp.broadcast_to(((slot_of >= N_SPLIT) & (slot_of < 2 * N_SPLIT))
                             .astype(F32)[:, None], (width, tm))
    return selq, onesq, selk, onesk


def kernel(x, c, emb_ln_g, emb_ln_b, w_ada, b_ada, w_in, rwkv_mix, w0, w_up, a0, a_up, k_k, k_a,
           r_k, gn_g, gn_b, fox_bf, w_out, ln_g, ln_b):
    bsz, t, d = x.shape
    depth = w_ada.shape[0]
    width = d // 2
    heads = width // HEAD_DIM
    rw_end = 3 * width + 2 * LORA
    fx_end = rw_end + 3 * width + heads
    alpha = (2 * depth) ** 0.25
    tm = min(TILE, t)

    c8 = jnp.pad(c, ((0, SUBLANES - bsz), (0, 0)))
    mods = _ada_call(c8, w_ada, b_ada)[:, :bsz]
    selq, onesq, selk, onesk = _fox_tables(heads, tm)

    h = x
    emb_g, emb_b = emb_ln_g.reshape(1, d), emb_ln_b.reshape(1, d)
    for l in range(depth):
        shift = mods[l, :, 0:d].reshape(bsz, 1, d)
        scale = mods[l, :, d:2 * d].reshape(bsz, 1, d)
        gate_c = mods[l, :, 2 * d:].reshape(bsz, 1, d)
        w = w_in[l]
        w_rw = w[:, :rw_end].astype(BF16)
        w_q = (w[:, rw_end:rw_end + width] * (HEAD_DIM ** -0.5 * LOG2E)).astype(BF16)
        w_kt = w[:, rw_end + width:rw_end + 2 * width].T.astype(BF16)
        w_v = w[:, rw_end + 2 * width:rw_end + 3 * width].astype(BF16)
        w_f = jnp.pad(w[:, rw_end + 3 * width:fx_end], ((0, 0), (0, LANES - heads))).astype(BF16)
        w_g = w[:, fx_end:].astype(BF16)
        bf = jnp.pad(fox_bf[l], (0, LANES - heads)).reshape(1, LANES)
        outs = _proj_call(h, scale, shift, emb_g, emb_b, w_rw, w_q, w_kt, w_v, w_g, w_f, bf,
                          selq, onesq, selk, onesk, tm, normalize=(l == 0))
        rw, q, kt, v, g, qb, kbt = outs[:7]
        if l == 0:
            h = outs[7]

        pp, mixwa, wcomb = _layer_tables(rwkv_mix[l], w0[l], w_up[l], a0[l], a_up[l], k_k[l],
                                         k_a[l], r_k[l], gn_g[l], gn_b[l], width)
        ya = _rwkv_call(rw, pp, mixwa, wcomb, width, tm)
        yb = _fox_call(q, qb, kt, kbt, v)
        wo = w_out[l].astype(BF16)
        h = _merge_call(ya, yb, g, h, gate_c, wo[:width], wo[width:], ln_g[l].reshape(1, d),
                        ln_b[l].reshape(1, d), alpha, tm)
    return h
```

```python
import functools

import jax
import jax.numpy as jnp
from jax import lax
from jax.experimental import pallas as pl
from jax.experimental.pallas import tpu as pltpu

F32 = jnp.float32
BF16 = jnp.bfloat16

HEAD_DIM = 64
LANES = 128
SUBLANES = 8
PAIR = LANES // HEAD_DIM
LORA = 64
LN_EPS = 1e-5
GN_EPS = 64e-5
CHUNK = 64
TILE = 512
N_SPLIT = 3
FOX_Q_BLOCKS = 2
FOX_MARGIN = 64.0
LOG2E = 1.4426950408889634
VMEM_LIMIT = 56 * 1024 * 1024


def _params(sem):
    return pltpu.CompilerParams(dimension_semantics=sem, vmem_limit_bytes=VMEM_LIMIT)


def _dot(a, b):
    return jnp.dot(a.astype(BF16), b.astype(BF16), preferred_element_type=F32)


def _dot_nt(a, b):
    return lax.dot_general(a.astype(BF16), b.astype(BF16), (((1,), (1,)), ((), ())),
                           preferred_element_type=F32)


def _split3(x):
    hi = x.astype(BF16)
    r1 = x - hi.astype(F32)
    mid = r1.astype(BF16)
    lo = (r1 - mid.astype(F32)).astype(BF16)
    return hi, mid, lo


def _dot_exact_rhs(a_bf16, x):
    hi, mid, lo = _split3(x)
    f = lambda y: jnp.dot(a_bf16, y, preferred_element_type=F32)
    return f(hi) + f(mid) + f(lo)


def _dot_exact_lhs(x, a_bf16):
    hi, mid, lo = _split3(x)
    f = lambda y: jnp.dot(y, a_bf16, preferred_element_type=F32)
    return f(hi) + f(mid) + f(lo)


def _softplus(z):
    return jnp.maximum(z, 0.0) + jnp.log(1.0 + jnp.exp(-jnp.abs(z)))


def _layer_norm_rows(x, g, b):
    mu = jnp.mean(x, axis=-1, keepdims=True)
    d = x - mu
    var = jnp.mean(d * d, axis=-1, keepdims=True)
    return d * lax.rsqrt(var + LN_EPS) * g + b


def _ada_kernel(c_ref, w_ref, b_ref, o_ref):
    c = c_ref[...]
    w = w_ref[0]
    ch, cm, cl = _split3(c)
    wh, wm, wl = _split3(w)
    f = lambda a, b: jnp.dot(a, b, preferred_element_type=F32)
    acc = f(ch, wh) + (f(ch, wm) + f(cm, wh)) + (f(ch, wl) + f(cm, wm) + f(cl, wh))
    o_ref[0] = acc + b_ref[0]


def _ada_call(c8, w_ada, b_ada):
    depth, d, d3 = w_ada.shape
    return pl.pallas_call(
        _ada_kernel,
        name="ada",
        out_shape=jax.ShapeDtypeStruct((depth, SUBLANES, d3), F32),
        grid=(depth,),
        in_specs=[pl.BlockSpec((SUBLANES, d), lambda l: (0, 0)),
                  pl.BlockSpec((1, d, d3), lambda l: (l, 0, 0)),
                  pl.BlockSpec((1, 1, d3), lambda l: (l, 0, 0))],
        out_specs=pl.BlockSpec((1, SUBLANES, d3), lambda l: (l, 0, 0)),
        compiler_params=_params(("arbitrary",)),
    )(c8, w_ada, b_ada.reshape(depth, 1, d3))


def _proj_kernel(x_ref, sc_ref, sh_ref, lng_ref, lnb_ref, wrw_ref, wq_ref, wkt_ref, wv_ref, wg_ref,
                 wf_ref, bf_ref, selq_ref, onesq_ref, selk_ref, onesk_ref,
                 rw_ref, q_ref, kt_ref, v_ref, g_ref, qb_ref, kbt_ref, *rest, normalize):
    tm = x_ref.shape[1]
    carry_ref = rest[-1]

    @pl.when(pl.program_id(1) == 0)
    def _():
        carry_ref[...] = jnp.zeros_like(carry_ref)

    x = x_ref[0]
    if normalize:
        x = _layer_norm_rows(x, lng_ref[...], lnb_ref[...])
        rest[0][0] = x
    h = x * (1.0 + sc_ref[0]) + sh_ref[0]
    hb = h.astype(BF16)
    rw_ref[0] = jnp.dot(hb, wrw_ref[...], preferred_element_type=F32)
    g_ref[0] = jnp.dot(hb, wg_ref[...], preferred_element_type=F32).astype(BF16)
    q_ref[0] = jnp.dot(hb, wq_ref[...], preferred_element_type=F32).astype(BF16)
    v_ref[0] = jnp.dot(hb, wv_ref[...], preferred_element_type=F32).astype(BF16)
    nt = lambda a, b: lax.dot_general(a, b, (((1,), (1,)), ((), ())), preferred_element_type=F32)
    kt_ref[0, 0] = nt(wkt_ref[...], hb).astype(BF16)

    z = jnp.dot(hb, wf_ref[...], preferred_element_type=F32) + bf_ref[...]
    lf = -_softplus(-z) * LOG2E
    r = lax.broadcasted_iota(jnp.int32, (tm, tm), 0)
    c = lax.broadcasted_iota(jnp.int32, (tm, tm), 1)
    tri = jnp.where(r >= c, 1.0, 0.0).astype(BF16)
    cum = _dot_exact_rhs(tri, lf) + carry_ref[...]
    carry_ref[...] = cum[tm - 1:tm, :]

    pos = _split3(cum)
    neg = _split3(-cum)
    qb = onesq_ref[...]
    kbt = onesk_ref[...]
    for i in range(N_SPLIT):
        qb = qb + jnp.dot(pos[i], selq_ref[i], preferred_element_type=F32)
        kbt = kbt + nt(selk_ref[i], neg[i])
    qb_ref[0] = qb.astype(BF16)
    kbt_ref[0, 0] = kbt.astype(BF16)


def _proj_call(x, scale, shift, ln_g, ln_b, w_rw, w_q, w_kt, w_v, w_g, w_f, bf, selq, onesq, selk,
               onesk, tm, normalize):
    bsz, t, d = x.shape
    nt = t // tm
    n_rw, n_g, width = w_rw.shape[1], w_g.shape[1], w_q.shape[1]
    full = lambda a: pl.BlockSpec(a.shape, lambda i, j: (0,) * a.ndim)
    rows = lambda n: pl.BlockSpec((1, tm, n), lambda i, j: (i, j, 0))
    vec = pl.BlockSpec((1, 1, d), lambda i, j: (i, 0, 0))
    tiled = lambda n: pl.BlockSpec((1, 1, n, tm), lambda i, j: (i, j, 0, 0))
    out_shape = [jax.ShapeDtypeStruct((bsz, t, n_rw), F32),
                 jax.ShapeDtypeStruct((bsz, t, width), BF16),
                 jax.ShapeDtypeStruct((bsz, nt, width, tm), BF16),
                 jax.ShapeDtypeStruct((bsz, t, width), BF16),
                 jax.ShapeDtypeStruct((bsz, t, n_g), BF16),
                 jax.ShapeDtypeStruct((bsz, t, width), BF16),
                 jax.ShapeDtypeStruct((bsz, nt, width, tm), BF16)]
    out_specs = [rows(n_rw), rows(width), tiled(width), rows(width), rows(n_g), rows(width),
                 tiled(width)]
    if normalize:
        out_shape.append(jax.ShapeDtypeStruct(x.shape, F32))
        out_specs.append(rows(d))
    return pl.pallas_call(
        functools.partial(_proj_kernel, normalize=normalize),
        name="proj_ln" if normalize else "proj",
        out_shape=tuple(out_shape),
        grid=(bsz, nt),
        in_specs=[rows(d), vec, vec, full(ln_g), full(ln_b), full(w_rw), full(w_q), full(w_kt),
                  full(w_v), full(w_g), full(w_f), full(bf), full(selq), full(onesq), full(selk),
                  full(onesk)],
        out_specs=tuple(out_specs),
        scratch_shapes=[pltpu.VMEM((1, LANES), F32)],
        compiler_params=_params(("arbitrary", "arbitrary")),
    )(x, scale, shift, ln_g, ln_b, w_rw, w_q, w_kt, w_v, w_g, w_f, bf, selq, onesq, selk, onesk)


_P_MIX_R, _P_MIX_K, _P_MIX_V, _P_W0, _P_A0, _P_KK, _P_KA, _P_RK, _P_GNG, _P_GNB = range(10)
_P_ROWS = 16


def _rwkv_kernel(r_ref, k_ref, v_ref, wa_ref, pp_ref, mixwa_ref, wcomb_ref, o_ref,
                 carry_ref, state_ref, q1_s, q2_s, g_s, h_s, bonus_s):
    ct = r_ref.shape[1]
    n_chunks = ct // CHUNK
    C = CHUNK
    chunks = range(n_chunks)

    @pl.when(pl.program_id(2) == 0)
    def _():
        for ref in (carry_ref, state_ref, q1_s, q2_s, g_s, h_s, bonus_s):
            ref[...] = jnp.zeros_like(ref)

    pp = pp_ref[0]
    prow = lambda i: pp[i:i + 1, :]
    row_id = lax.broadcasted_iota(jnp.int32, (ct, LANES), 0)
    lane_id = lax.broadcasted_iota(jnp.int32, (ct, LANES), 1)

    def shifted(ref, slot, mix):
        x = ref[0]
        prev = pltpu.roll(x, 1, 0)
        prev = jnp.where(row_id == 0, carry_ref[slot:slot + 1, :], prev)
        carry_ref[slot:slot + 1, :] = x[ct - 1:ct, :]
        return x + (prev - x) * mix

    r = shifted(r_ref, 0, prow(_P_MIX_R))
    k = shifted(k_ref, 1, prow(_P_MIX_K))
    v = shifted(v_ref, 2, prow(_P_MIX_V))
    wa = shifted(wa_ref, 3, mixwa_ref[...])

    t_in = jnp.where(lane_id < LORA, jnp.tanh(wa), wa)
    dwa = _dot(t_in, wcomb_ref[0])
    w_log = -_softplus(-(prow(_P_W0) + dwa[:, :LANES])) - 0.5
    lw = -jnp.exp(w_log)
    a = 1.0 / (1.0 + jnp.exp(-(prow(_P_A0) + dwa[:, LANES:])))

    bd_r = lax.broadcasted_iota(jnp.int32, (LANES, LANES), 0)
    bd_c = lax.broadcasted_iota(jnp.int32, (LANES, LANES), 1)
    same_head = (bd_r // HEAD_DIM) == (bd_c // HEAD_DIM)
    ones_bd = jnp.where(same_head, 1.0, 0.0).astype(BF16)

    kk = k * prow(_P_KK)
    ss = _dot_exact_lhs(kk * kk, ones_bd)
    kap = kk / jnp.maximum(jnp.sqrt(ss), 1e-12)
    k2 = k * (1.0 + (a - 1.0) * prow(_P_KA))
    b = kap * a
    bonus = _dot_exact_lhs(r * k2 * prow(_P_RK), ones_bd) * v

    lane_c = lax.broadcasted_iota(jnp.int32, (C, LANES), 1)
    m1 = jnp.where(lane_c < HEAD_DIM, 1.0, 0.0)
    m2 = 1.0 - m1
    stack = lambda x: jnp.concatenate([x * m1, x * m2], axis=0)
    dup = lambda x: jnp.concatenate([x, x], axis=0)
    sm = jnp.concatenate([m1, m2], axis=0)
    strict_lower = same_head & ((bd_r % C) > (bd_c % C))
    lower = same_head & ((bd_r % C) >= (bd_c % C))
    eye = bd_r == bd_c
    eye_f = jnp.where(eye, 1.0, 0.0)
    tr = lax.broadcasted_iota(jnp.int32, (C, C), 0)
    tc = lax.broadcasted_iota(jnp.int32, (C, C), 1)
    tri = jnp.where(tr >= tc, 1.0, 0.0).astype(BF16)
    gn_g, gn_b = prow(_P_GNG), prow(_P_GNB)
    sl = lambda ci: slice(ci * C, (ci + 1) * C)

    state = [state_ref[...]]

    def emit_previous(ci):
        ys = _dot(q1_s[ci], state[0]) + q2_s[ci]
        state[0] = _dot(g_s[ci], state[0]) + h_s[ci]
        mu = jnp.sum(ys, axis=-1, keepdims=True) * (1.0 / HEAD_DIM)
        d = (ys - mu) * sm
        var = jnp.sum(d * d, axis=-1, keepdims=True) * (1.0 / HEAD_DIM)
        yn = d * lax.rsqrt(var + GN_EPS)
        o_ref[0, sl(ci), :] = ((yn[:C] + yn[C:]) * gn_g + gn_b
                               + bonus_s[sl(ci), :]).astype(o_ref.dtype)

    pending = list(chunks)

    def hook():
        if pending:
            emit_previous(pending.pop(0))

    L = [_dot_exact_rhs(tri, lw[sl(ci)]) for ci in chunks]
    Lc = [x[C - 1:C, :] for x in L]
    e_neg = [jnp.exp(-x) for x in L]
    e_end = [jnp.exp(xc - x) for xc, x in zip(Lc, L)]
    Ks = [stack(kap[sl(ci)] * jnp.exp(L[ci] - lw[sl(ci)])).astype(BF16) for ci in chunks]
    Rs = [stack(r[sl(ci)] * jnp.exp(L[ci])) for ci in chunks]
    Vs = [stack(v[sl(ci)]).astype(BF16) for ci in chunks]
    K2 = [dup(k2[sl(ci)] * e_neg[ci]).astype(BF16) for ci in chunks]
    B2 = [dup(b[sl(ci)] * e_neg[ci]).astype(BF16) for ci in chunks]
    KendT = [stack(k2[sl(ci)] * e_end[ci]).T.astype(BF16) for ci in chunks]
    BendT = [stack(b[sl(ci)] * e_end[ci]).T.astype(BF16) for ci in chunks]
    N = [jnp.where(strict_lower, _dot_nt(Ks[ci], B2[ci]), 0.0) for ci in chunks]
    Akk = [jnp.where(strict_lower, _dot_nt(Ks[ci], K2[ci]), 0.0).astype(BF16) for ci in chunks]
    hook()
    Ark = [jnp.where(lower, _dot_nt(Rs[ci], K2[ci]), 0.0).astype(BF16) for ci in chunks]
    Arb = [jnp.where(lower, _dot_nt(Rs[ci], B2[ci]), 0.0).astype(BF16) for ci in chunks]
    hook()
    AkkV = [_dot(Akk[ci], Vs[ci]) for ci in chunks]
    ArkV = [_dot(Ark[ci], Vs[ci]) for ci in chunks]
    KV = [_dot(KendT[ci], Vs[ci]) for ci in chunks]
    hook()
    P = [_dot(x, x) for x in N]
    T = [eye_f - x for x in N]
    for _ in range(C.bit_length() - 3):
        both = [_dot(jnp.concatenate([t, p], axis=0), p) for t, p in zip(T, P)]
        T = [t + x[:2 * C] for t, x in zip(T, both)]
        P = [x[2 * C:] for x in both]
        hook()
    Tb = [(t + _dot(t, p)).astype(BF16) for t, p in zip(T, P)]
    hook()
    P1 = [_dot(Tb[ci], Ks[ci]).astype(BF16) for ci in chunks]
    P2 = [_dot(Tb[ci], AkkV[ci]).astype(BF16) for ci in chunks]
    while pending:
        hook()
    state_ref[...] = state[0]
    for ci in chunks:
        q1_s[ci] = Rs[ci] - _dot(Arb[ci], P1[ci])
        q2_s[ci] = ArkV[ci] - _dot(Arb[ci], P2[ci])
        g_s[ci] = (jnp.where(eye, jnp.broadcast_to(jnp.exp(Lc[ci]), (LANES, LANES)), 0.0)
                   - _dot(BendT[ci], P1[ci]))
        h_s[ci] = KV[ci] - _dot(BendT[ci], P2[ci])
    bonus_s[...] = bonus


def _rwkv_call(rw, pp, mixwa, wcomb, width, ct):
    bsz, t, _ = rw.shape
    n_pairs = width // LANES
    nt = t // ct
    last = nt - 1
    blk = lambda off: pl.BlockSpec((1, ct, LANES),
                                   lambda i, p, j, off=off: (i, jnp.minimum(j, last), off + p))
    wa_blk = pl.BlockSpec((1, ct, LANES), lambda i, p, j: (i, jnp.minimum(j, last), 3 * n_pairs))
    staged = pltpu.VMEM((ct // CHUNK, LANES, LANES), F32)
    return pl.pallas_call(
        _rwkv_kernel,
        name="rwkv",
        out_shape=jax.ShapeDtypeStruct((bsz, t, width), BF16),
        grid=(bsz, n_pairs, nt + 1),
        in_specs=[blk(0), blk(n_pairs), blk(2 * n_pairs), wa_blk,
                  pl.BlockSpec((1, _P_ROWS, LANES), lambda i, p, j: (p, 0, 0)),
                  pl.BlockSpec((1, LANES), lambda i, p, j: (0, 0)),
                  pl.BlockSpec((1, LANES, 2 * LANES), lambda i, p, j: (p, 0, 0))],
        out_specs=pl.BlockSpec((1, ct, LANES), lambda i, p, j: (i, jnp.maximum(j - 1, 0), p)),
        scratch_shapes=[pltpu.VMEM((SUBLANES, LANES), F32), pltpu.VMEM((LANES, LANES), F32),
                        staged, staged, staged, staged, pltpu.VMEM((ct, LANES), F32)],
        compiler_params=_params(("arbitrary", "arbitrary", "arbitrary")),
    )(rw, rw, rw, rw, pp, mixwa, wcomb)


def _fox_kernel(q_ref, qb_ref, kt_ref, kbt_ref, v_ref, o_ref,
                s_buf, p_buf, m_buf, a_buf, acc_buf, tab_ref):
    bq, bk = q_ref.shape[1], kt_ref.shape[3]
    nt = kt_ref.shape[1]
    qi = pl.program_id(2)
    n_full = qi * FOX_Q_BLOCKS
    heads = range(PAIR)
    lane_q = lax.broadcasted_iota(jnp.int32, (bq, LANES), 1)
    lane_v = lax.broadcasted_iota(jnp.int32, (bk, LANES), 1)
    lane_t = lax.broadcasted_iota(jnp.int32, (1, LANES), 1)
    own_q = (lane_q < HEAD_DIM, lane_q >= HEAD_DIM)
    own_v = (lane_v < HEAD_DIM, lane_v >= HEAD_DIM)
    den_lane = (HEAD_DIM, 0)
    table_at = (HEAD_DIM, 0)
    q, qb = q_ref[0], qb_ref[0]
    q_aug = [jnp.where(own_q[h], q, qb) for h in heads]
    v_ones = [jnp.where(lane_v == den_lane[h], 1.0, 0.0).astype(BF16) for h in heads]
    rr = lax.broadcasted_iota(jnp.int32, (bq, bk), 0)
    cc = lax.broadcasted_iota(jnp.int32, (bq, bk), 1)

    @pl.when(qi == 0)
    def _():
        def one(j, tabs):
            kt = kt_ref[0, j].astype(F32)
            kbt = kbt_ref[0, j].astype(F32)
            sq = kt * kt
            new = []
            for h in heads:
                ssq = jnp.sum(sq[h * HEAD_DIM:(h + 1) * HEAD_DIM], axis=0, keepdims=True)
                new.append(jnp.sqrt(jnp.max(ssq, axis=1, keepdims=True)))
            for h in heads:
                t0 = table_at[h]
                ck = -(kbt[t0:t0 + 1] + kbt[t0 + 1:t0 + 2] + kbt[t0 + 2:t0 + 3])
                new.append(jnp.min(ck, axis=1, keepdims=True))
            return tuple(jnp.where(lane_t == j, x, t) for x, t in zip(new, tabs))

        zero = jnp.zeros((1, LANES), F32)
        tabs = lax.fori_loop(0, nt, one, (zero,) * (2 * PAIR))
        for i, t in enumerate(tabs):
            tab_ref[i:i + 1, :] = t

    bd_r = lax.broadcasted_iota(jnp.int32, (LANES, LANES), 0)
    bd_c = lax.broadcasted_iota(jnp.int32, (LANES, LANES), 1)
    ones_bd = jnp.where((bd_r // HEAD_DIM) == (bd_c // HEAD_DIM), 1.0, 0.0).astype(BF16)
    qf, qbf = q.astype(F32), qb.astype(F32)
    qn_lanes = jnp.sqrt(jnp.max(_dot_exact_lhs(qf * qf, ones_bd), axis=0, keepdims=True))
    qn = [qn_lanes[:, h * HEAD_DIM:h * HEAD_DIM + 1] for h in heads]
    cq_max = []
    for h in heads:
        t0 = table_at[h] + N_SPLIT
        cq = qbf[:, t0:t0 + 1] + qbf[:, t0 + 1:t0 + 2] + qbf[:, t0 + 2:t0 + 3]
        cq_max.append(jnp.max(cq, axis=0, keepdims=True))

    wide = lambda x: jnp.concatenate([x] * (bk // LANES), axis=1)

    def scores(j, slot, m_prev, diag=None):
        kt, kbt = kt_ref[0, j], kbt_ref[0, j]
        k_aug = (jnp.concatenate([kt[:HEAD_DIM], kbt[HEAD_DIM:]], axis=0),
                 jnp.concatenate([kbt[:HEAD_DIM], kt[HEAD_DIM:]], axis=0))
        for h in heads:
            s = jnp.dot(q_aug[h], k_aug[h], preferred_element_type=F32)
            if diag is not None:
                s = jnp.where(cc + diag * bk <= rr, s, -jnp.inf)
            s_buf[slot, h] = s
            bmax = jnp.broadcast_to(jnp.max(s, axis=1, keepdims=True), (bq, LANES))
            m_buf[slot, h] = jnp.maximum(m_prev[h], bmax)

    def softmax(slot, m_prev):
        for h in heads:
            m_cur = m_buf[slot, h]
            a_buf[slot, h] = jnp.exp2(m_prev[h] - m_cur)
            p_buf[slot, h] = jnp.exp2(s_buf[slot, h] - wide(m_cur)).astype(BF16)

    def values(j, slot):
        vb = v_ref[0, pl.ds(pl.multiple_of(j * bk, bk), bk), :]
        for h in heads:
            v_aug = jnp.where(own_v[h], vb, v_ones[h])
            acc_buf[h] = a_buf[slot, h] * acc_buf[h] + jnp.dot(p_buf[slot, h], v_aug,
                                                               preferred_element_type=F32)

    def step(j, slot):
        m_before = [m_buf[1 - slot, h] for h in heads]
        scores(j + 1, 1 - slot, [m_buf[slot, h] for h in heads])
        values(jnp.maximum(j - 1, 0), 1 - slot)
        softmax(slot, m_before)

    def drain(last, slot):
        values(jnp.maximum(last - 1, 0), 1 - slot)
        softmax(slot, [m_buf[1 - slot, h] for h in heads])
        values(last, slot)

    never = [jnp.full((bq, LANES), -jnp.inf, F32)] * PAIR
    acc_buf[...] = jnp.zeros_like(acc_buf)
    scores(n_full, 0, never, diag=0)
    softmax(0, never)
    lo = slice(bk, bq)
    kt2, kbt2 = kt_ref[0, n_full + 1], kbt_ref[0, n_full + 1]
    k_aug2 = (jnp.concatenate([kt2[:HEAD_DIM], kbt2[HEAD_DIM:]], axis=0),
              jnp.concatenate([kbt2[:HEAD_DIM], kt2[HEAD_DIM:]], axis=0))
    tri = (lax.broadcasted_iota(jnp.int32, (bk, bk), 1)
           <= lax.broadcasted_iota(jnp.int32, (bk, bk), 0))
    for h in heads:
        s = jnp.dot(q_aug[h][lo], k_aug2[h], preferred_element_type=F32)
        s_buf[1, h, lo, :] = jnp.where(tri, s, -jnp.inf)
    values(n_full, 0)
    vb2 = v_ref[0, pl.ds(pl.multiple_of((n_full + 1) * bk, bk), bk), :]
    for h in heads:
        m_old = m_buf[0, h, lo, :]
        s = s_buf[1, h, lo, :]
        bmax = jnp.broadcast_to(jnp.max(s, axis=1, keepdims=True), (bk, LANES))
        m_lo = jnp.maximum(m_old, bmax)
        p = jnp.exp2(s - wide(m_lo)).astype(BF16)
        v_aug = jnp.where(own_v[h], vb2, v_ones[h])
        acc_buf[h, lo, :] = (jnp.exp2(m_old - m_lo) * acc_buf[h, lo, :]
                             + jnp.dot(p, v_aug, preferred_element_type=F32))
        m_buf[1, h, :bk, :] = m_buf[0, h, :bk, :]
        m_buf[1, h, lo, :] = m_lo

    needed = lane_t < 0
    for h in heads:
        m_min = jnp.min(m_buf[1, h], axis=0, keepdims=True)
        bound = (qn[h] * tab_ref[h:h + 1, :] * (1.0 + 2.0 ** -8) + 1.0
                 + cq_max[h] - tab_ref[PAIR + h:PAIR + h + 1, :])
        needed = jnp.logical_or(needed, bound > m_min - FOX_MARGIN)
    needed = jnp.logical_and(needed, lane_t < n_full)
    first = jnp.min(jnp.where(needed, lane_t, n_full))
    count = n_full - first

    p_buf[1] = jnp.zeros(p_buf.shape[1:], BF16)
    a_buf[1] = jnp.ones(a_buf.shape[1:], F32)

    @pl.when(count > 0)
    def _():
        scores(first, 0, [m_buf[1, h] for h in heads])

    def pair(jj, carry):
        step(first + 2 * jj, 0)
        step(first + 2 * jj + 1, 1)
        return carry

    lax.fori_loop(0, jnp.maximum(count - 1, 0) // 2, pair, 0)
    even = count % 2 == 0

    @pl.when(jnp.logical_and(count >= 2, even))
    def _():
        step(n_full - 2, 0)
        drain(n_full - 1, 1)

    @pl.when(jnp.logical_not(even))
    def _():
        drain(n_full - 1, 0)

    out = [acc_buf[h] / acc_buf[h][:, den_lane[h]:den_lane[h] + 1] for h in heads]
    o_ref[0] = jnp.where(own_q[0], out[0], out[1]).astype(o_ref.dtype)


def _fox_call(q, qb, kt, kbt, v):
    bsz, t, width = q.shape
    nt, bk = kt.shape[1], kt.shape[3]
    bq = FOX_Q_BLOCKS * bk
    n_pairs = width // LANES
    q_blk = pl.BlockSpec((1, bq, LANES), lambda i, p, j: (i, j, p))
    once = pl.Buffered(1)
    kt_blk = pl.BlockSpec((1, nt, LANES, bk), lambda i, p, j: (i, 0, p, 0), pipeline_mode=once)
    v_blk = pl.BlockSpec((1, t, LANES), lambda i, p, j: (i, 0, p), pipeline_mode=once)
    return pl.pallas_call(
        _fox_kernel,
        name="fox",
        out_shape=jax.ShapeDtypeStruct((bsz, t, width), BF16),
        grid=(bsz, n_pairs, t // bq),
        in_specs=[q_blk, q_blk, kt_blk, kt_blk, v_blk],
        out_specs=q_blk,
        scratch_shapes=[pltpu.VMEM((2, PAIR, bq, bk), F32), pltpu.VMEM((2, PAIR, bq, bk), BF16),
                        pltpu.VMEM((2, PAIR, bq, LANES), F32), pltpu.VMEM((2, PAIR, bq, LANES), F32),
                        pltpu.VMEM((PAIR, bq, LANES), F32), pltpu.VMEM((SUBLANES, LANES), F32)],
        compiler_params=_params(("arbitrary", "arbitrary", "arbitrary")),
    )(q, qb, kt, kbt, v)


def _merge_kernel(ya_ref, yb_ref, g_ref, x_ref, gc_ref, wa_ref, wb_ref, lg_ref, lb_ref, o_ref,
                  *, alpha):
    half = ya_ref.shape[2]
    g = g_ref[0].astype(F32)
    silu = g / (1.0 + jnp.exp(-g))
    out = (_dot(ya_ref[0].astype(F32) * silu[:, :half], wa_ref[...])
           + _dot(yb_ref[0].astype(F32) * silu[:, half:], wb_ref[...]))
    z = alpha * x_ref[0] + (1.0 + gc_ref[0]) * out
    o_ref[0] = _layer_norm_rows(z, lg_ref[...], lb_ref[...])


def _merge_call(ya, yb, g, x, gate_c, w_a, w_b, ln_g, ln_b, alpha, tm):
    bsz, t, d = x.shape
    half = ya.shape[2]
    rows = lambda n: pl.BlockSpec((1, tm, n), lambda i, j: (i, j, 0))
    full = lambda a: pl.BlockSpec(a.shape, lambda i, j: (0,) * a.ndim)
    return pl.pallas_call(
        functools.partial(_merge_kernel, alpha=alpha),
        name="merge",
        out_shape=jax.ShapeDtypeStruct(x.shape, F32),
        grid=(bsz, t // tm),
        in_specs=[rows(half), rows(half), rows(g.shape[2]), rows(d),
                  pl.BlockSpec((1, 1, d), lambda i, j: (i, 0, 0)),
                  full(w_a), full(w_b), full(ln_g), full(ln_b)],
        out_specs=rows(d),
        compiler_params=_params(("arbitrary", "arbitrary")),
    )(ya, yb, g, x, gate_c, w_a, w_b, ln_g, ln_b)


def _layer_tables(rwkv_mix, w0, w_up, a0, a_up, k_k, k_a, r_k, gn_g, gn_b, width):
    n_pairs = width // LANES
    per_pair = lambda vct: vct.reshape(n_pairs, 1, LANES)
    rows = [rwkv_mix[0:width], rwkv_mix[width:2 * width], rwkv_mix[2 * width:3 * width],
            w0, a0, k_k, k_a, r_k, gn_g, gn_b]
    pp = jnp.concatenate([per_pair(x) for x in rows]
                         + [jnp.zeros((n_pairs, _P_ROWS - len(rows), LANES), F32)], axis=1)
    mixwa = rwkv_mix[3 * width:].reshape(1, 2 * LORA)
    zeros = jnp.zeros((n_pairs, LORA, LANES), F32)
    wu = w_up.reshape(LORA, n_pairs, LANES).transpose(1, 0, 2)
    au = a_up.reshape(LORA, n_pairs, LANES).transpose(1, 0, 2)
    wcomb = jnp.concatenate([jnp.concatenate([wu, zeros], axis=2),
                             jnp.concatenate([zeros, au], axis=2)], axis=1).astype(BF16)
    return pp, mixwa, wcomb


def _fox_tables(heads, tm):
    width = heads * HEAD_DIM
    chan = jnp.arange(width)
    off = chan % LANES
    src = jnp.arange(LANES)[:, None]
    base = lambda hd: (hd // PAIR) * LANES + jnp.where(hd % PAIR == 0, HEAD_DIM, 0)
    at = lambda slot: (chan[None, :] == base(src) + slot) & (src < heads)
    selq = jnp.stack([at(N_SPLIT + i) for i in range(N_SPLIT)]).astype(BF16)
    selk = jnp.stack([at(i).T for i in range(N_SPLIT)]).astype(BF16)
    slot_of = off % HEAD_DIM
    onesq = (slot_of < N_SPLIT).astype(F32).reshape(1, width)
    onesk = jnp.broadcast_to(((slot_of >= N_SPLIT) & (slot_of < 2 * N_SPLIT))
                             .astype(F32)[:, None], (width, tm))
    return selq, onesq, selk, onesk


def kernel(x, c, emb_ln_g, emb_ln_b, w_ada, b_ada, w_in, rwkv_mix, w0, w_up, a0, a_up, k_k, k_a,
           r_k, gn_g, gn_b, fox_bf, w_out, ln_g, ln_b):
    bsz, t, d = x.shape
    depth = w_ada.shape[0]
    width = d // 2
    heads = width // HEAD_DIM
    rw_end = 3 * width + 2 * LORA
    fx_end = rw_end + 3 * width + heads
    alpha = (2 * depth) ** 0.25
    tm = min(TILE, t)

    c8 = jnp.pad(c, ((0, SUBLANES - bsz), (0, 0)))
    mods = _ada_call(c8, w_ada, b_ada)[:, :bsz]
    selq, onesq, selk, onesk = _fox_tables(heads, tm)

    h = x
    emb_g, emb_b = emb_ln_g.reshape(1, d), emb_ln_b.reshape(1, d)
    for l in range(depth):
        shift = mods[l, :, 0:d].reshape(bsz, 1, d)
        scale = mods[l, :, d:2 * d].reshape(bsz, 1, d)
        gate_c = mods[l, :, 2 * d:].reshape(bsz, 1, d)
        w = w_in[l]
        w_rw = w[:, :rw_end].astype(BF16)
        w_q = (w[:, rw_end:rw_end + width] * (HEAD_DIM ** -0.5 * LOG2E)).astype(BF16)
        w_kt = w[:, rw_end + width:rw_end + 2 * width].T.astype(BF16)
        w_v = w[:, rw_end + 2 * width:rw_end + 3 * width].astype(BF16)
        w_f = jnp.pad(w[:, rw_end + 3 * width:fx_end], ((0, 0), (0, LANES - heads))).astype(BF16)
        w_g = w[:, fx_end:].astype(BF16)
        bf = jnp.pad(fox_bf[l], (0, LANES - heads)).reshape(1, LANES)
        outs = _proj_call(h, scale, shift, emb_g, emb_b, w_rw, w_q, w_kt, w_v, w_g, w_f, bf,
                          selq, onesq, selk, onesk, tm, normalize=(l == 0))
        rw, q, kt, v, g, qb, kbt = outs[:7]
        if l == 0:
            h = outs[7]

        pp, mixwa, wcomb = _layer_tables(rwkv_mix[l], w0[l], w_up[l], a0[l], a_up[l], k_k[l],
                                         k_a[l], r_k[l], gn_g[l], gn_b[l], width)
        ya = _rwkv_call(rw, pp, mixwa, wcomb, width, tm)
        yb = _fox_call(q, qb, kt, kbt, v)
        wo = w_out[l].astype(BF16)
        h = _merge_call(ya, yb, g, h, gate_c, wo[:width], wo[width:], ln_g[l].reshape(1, d),
                        ln_b[l].reshape(1, d), alpha, tm)
    return h
```

```python
import functools

import jax
import jax.numpy as jnp
from jax import lax
from jax.experimental import pallas as pl
from jax.experimental.pallas import tpu as pltpu

F32 = jnp.float32
BF16 = jnp.bfloat16

HEAD_DIM = 64
LANES = 128
SUBLANES = 8
PAIR = LANES // HEAD_DIM
LORA = 64
LN_EPS = 1e-5
GN_EPS = 64e-5
CHUNK = 64
TILE = 512
N_SPLIT = 3
FOX_Q_BLOCKS = 2
FOX_MARGIN = 48.0
LOG2E = 1.4426950408889634
VMEM_LIMIT = 56 * 1024 * 1024


def _params(sem):
    return pltpu.CompilerParams(dimension_semantics=sem, vmem_limit_bytes=VMEM_LIMIT)


def _dot(a, b):
    return jnp.dot(a.astype(BF16), b.astype(BF16), preferred_element_type=F32)


def _dot_nt(a, b):
    return lax.dot_general(a.astype(BF16), b.astype(BF16), (((1,), (1,)), ((), ())),
                           preferred_element_type=F32)


def _split3(x):
    hi = x.astype(BF16)
    r1 = x - hi.astype(F32)
    mid = r1.astype(BF16)
    lo = (r1 - mid.astype(F32)).astype(BF16)
    return hi, mid, lo


def _dot_exact_rhs(a_bf16, x):
    hi, mid, lo = _split3(x)
    f = lambda y: jnp.dot(a_bf16, y, preferred_element_type=F32)
    return f(hi) + f(mid) + f(lo)


def _dot_exact_lhs(x, a_bf16):
    hi, mid, lo = _split3(x)
    f = lambda y: jnp.dot(y, a_bf16, preferred_element_type=F32)
    return f(hi) + f(mid) + f(lo)


def _softplus(z):
    return jnp.maximum(z, 0.0) + jnp.log(1.0 + jnp.exp(-jnp.abs(z)))


def _layer_norm_rows(x, g, b):
    mu = jnp.mean(x, axis=-1, keepdims=True)
    d = x - mu
    var = jnp.mean(d * d, axis=-1, keepdims=True)
    return d * lax.rsqrt(var + LN_EPS) * g + b


def _ada_kernel(c_ref, w_ref, b_ref, o_ref):
    c = c_ref[...]
    w = w_ref[0]
    ch, cm, cl = _split3(c)
    wh, wm, wl = _split3(w)
    f = lambda a, b: jnp.dot(a, b, preferred_element_type=F32)
    acc = f(ch, wh) + (f(ch, wm) + f(cm, wh)) + (f(ch, wl) + f(cm, wm) + f(cl, wh))
    o_ref[0] = acc + b_ref[0]


def _ada_call(c8, w_ada, b_ada):
    depth, d, d3 = w_ada.shape
    return pl.pallas_call(
        _ada_kernel,
        name="ada",
        out_shape=jax.ShapeDtypeStruct((depth, SUBLANES, d3), F32),
        grid=(depth,),
        in_specs=[pl.BlockSpec((SUBLANES, d), lambda l: (0, 0)),
                  pl.BlockSpec((1, d, d3), lambda l: (l, 0, 0)),
                  pl.BlockSpec((1, 1, d3), lambda l: (l, 0, 0))],
        out_specs=pl.BlockSpec((1, SUBLANES, d3), lambda l: (l, 0, 0)),
        compiler_params=_params(("arbitrary",)),
    )(c8, w_ada, b_ada.reshape(depth, 1, d3))


def _proj_kernel(x_ref, sc_ref, sh_ref, lng_ref, lnb_ref, wrw_ref, wq_ref, wkt_ref, wv_ref, wg_ref,
                 wf_ref, bf_ref, selq_ref, onesq_ref, selk_ref, onesk_ref,
                 rw_ref, q_ref, kt_ref, v_ref, g_ref, qb_ref, kbt_ref, *rest, normalize):
    tm = x_ref.shape[1]
    carry_ref = rest[-1]

    @pl.when(pl.program_id(1) == 0)
    def _():
        carry_ref[...] = jnp.zeros_like(carry_ref)

    x = x_ref[0]
    if normalize:
        x = _layer_norm_rows(x, lng_ref[...], lnb_ref[...])
        rest[0][0] = x
    h = x * (1.0 + sc_ref[0]) + sh_ref[0]
    hb = h.astype(BF16)
    rw_ref[0] = jnp.dot(hb, wrw_ref[...], preferred_element_type=F32)
    g_ref[0] = jnp.dot(hb, wg_ref[...], preferred_element_type=F32).astype(BF16)
    q_ref[0] = jnp.dot(hb, wq_ref[...], preferred_element_type=F32).astype(BF16)
    v_ref[0] = jnp.dot(hb, wv_ref[...], preferred_element_type=F32).astype(BF16)
    nt = lambda a, b: lax.dot_general(a, b, (((1,), (1,)), ((), ())), preferred_element_type=F32)
    kt_ref[0, 0] = nt(wkt_ref[...], hb).astype(BF16)

    z = jnp.dot(hb, wf_ref[...], preferred_element_type=F32) + bf_ref[...]
    lf = -_softplus(-z) * LOG2E
    r = lax.broadcasted_iota(jnp.int32, (tm, tm), 0)
    c = lax.broadcasted_iota(jnp.int32, (tm, tm), 1)
    tri = jnp.where(r >= c, 1.0, 0.0).astype(BF16)
    cum = _dot_exact_rhs(tri, lf) + carry_ref[...]
    carry_ref[...] = cum[tm - 1:tm, :]

    pos = _split3(cum)
    neg = _split3(-cum)
    qb = onesq_ref[...]
    kbt = onesk_ref[...]
    for i in range(N_SPLIT):
        qb = qb + jnp.dot(pos[i], selq_ref[i], preferred_element_type=F32)
        kbt = kbt + nt(selk_ref[i], neg[i])
    qb_ref[0] = qb.astype(BF16)
    kbt_ref[0, 0] = kbt.astype(BF16)


def _proj_call(x, scale, shift, ln_g, ln_b, w_rw, w_q, w_kt, w_v, w_g, w_f, bf, selq, onesq, selk,
               onesk, tm, normalize):
    bsz, t, d = x.shape
    nt = t // tm
    n_rw, n_g, width = w_rw.shape[1], w_g.shape[1], w_q.shape[1]
    full = lambda a: pl.BlockSpec(a.shape, lambda i, j: (0,) * a.ndim)
    rows = lambda n: pl.BlockSpec((1, tm, n), lambda i, j: (i, j, 0))
    vec = pl.BlockSpec((1, 1, d), lambda i, j: (i, 0, 0))
    tiled = lambda n: pl.BlockSpec((1, 1, n, tm), lambda i, j: (i, j, 0, 0))
    out_shape = [jax.ShapeDtypeStruct((bsz, t, n_rw), F32),
                 jax.ShapeDtypeStruct((bsz, t, width), BF16),
                 jax.ShapeDtypeStruct((bsz, nt, width, tm), BF16),
                 jax.ShapeDtypeStruct((bsz, t, width), BF16),
                 jax.ShapeDtypeStruct((bsz, t, n_g), BF16),
                 jax.ShapeDtypeStruct((bsz, t, width), BF16),
                 jax.ShapeDtypeStruct((bsz, nt, width, tm), BF16)]
    out_specs = [rows(n_rw), rows(width), tiled(width), rows(width), rows(n_g), rows(width),
                 tiled(width)]
    if normalize:
        out_shape.append(jax.ShapeDtypeStruct(x.shape, F32))
        out_specs.append(rows(d))
    return pl.pallas_call(
        functools.partial(_proj_kernel, normalize=normalize),
        name="proj_ln" if normalize else "proj",
        out_shape=tuple(out_shape),
        grid=(bsz, nt),
        in_specs=[rows(d), vec, vec, full(ln_g), full(ln_b), full(w_rw), full(w_q), full(w_kt),
                  full(w_v), full(w_g), full(w_f), full(bf), full(selq), full(onesq), full(selk),
                  full(onesk)],
        out_specs=tuple(out_specs),
        scratch_shapes=[pltpu.VMEM((1, LANES), F32)],
        compiler_params=_params(("arbitrary", "arbitrary")),
    )(x, scale, shift, ln_g, ln_b, w_rw, w_q, w_kt, w_v, w_g, w_f, bf, selq, onesq, selk, onesk)


_P_MIX_R, _P_MIX_K, _P_MIX_V, _P_W0, _P_A0, _P_KK, _P_KA, _P_RK, _P_GNG, _P_GNB = range(10)
_P_ROWS = 16


def _rwkv_kernel(r_ref, k_ref, v_ref, wa_ref, pp_ref, mixwa_ref, wcomb_ref, o_ref,
                 carry_ref, state_ref, q1g_s, q2_s, h_s, bonus_s):
    ct = r_ref.shape[1]
    n_chunks = ct // CHUNK
    C = CHUNK
    chunks = range(n_chunks)

    @pl.when(pl.program_id(2) == 0)
    def _():
        for ref in (carry_ref, state_ref, q1g_s, q2_s, h_s, bonus_s):
            ref[...] = jnp.zeros_like(ref)

    pp = pp_ref[0]
    prow = lambda i: pp[i:i + 1, :]
    row_id = lax.broadcasted_iota(jnp.int32, (ct, LANES), 0)
    lane_id = lax.broadcasted_iota(jnp.int32, (ct, LANES), 1)

    def shifted(ref, slot, mix):
        x = ref[0]
        prev = pltpu.roll(x, 1, 0)
        prev = jnp.where(row_id == 0, carry_ref[slot:slot + 1, :], prev)
        carry_ref[slot:slot + 1, :] = x[ct - 1:ct, :]
        return x + (prev - x) * mix

    r = shifted(r_ref, 0, prow(_P_MIX_R))
    k = shifted(k_ref, 1, prow(_P_MIX_K))
    v = shifted(v_ref, 2, prow(_P_MIX_V))
    wa = shifted(wa_ref, 3, mixwa_ref[...])

    t_in = jnp.where(lane_id < LORA, jnp.tanh(wa), wa)
    dwa = _dot(t_in, wcomb_ref[0])
    w_log = -_softplus(-(prow(_P_W0) + dwa[:, :LANES])) - 0.5
    lw = -jnp.exp(w_log)
    a = 1.0 / (1.0 + jnp.exp(-(prow(_P_A0) + dwa[:, LANES:])))

    bd_r = lax.broadcasted_iota(jnp.int32, (LANES, LANES), 0)
    bd_c = lax.broadcasted_iota(jnp.int32, (LANES, LANES), 1)
    same_head = (bd_r // HEAD_DIM) == (bd_c // HEAD_DIM)
    ones_bd = jnp.where(same_head, 1.0, 0.0).astype(BF16)

    kk = k * prow(_P_KK)
    ss = _dot_exact_lhs(kk * kk, ones_bd)
    kap = kk / jnp.maximum(jnp.sqrt(ss), 1e-12)
    k2 = k * (1.0 + (a - 1.0) * prow(_P_KA))
    b = kap * a
    bonus = _dot_exact_lhs(r * k2 * prow(_P_RK), ones_bd) * v

    lane_c = lax.broadcasted_iota(jnp.int32, (C, LANES), 1)
    m1 = jnp.where(lane_c < HEAD_DIM, 1.0, 0.0)
    m2 = 1.0 - m1
    stack = lambda x: jnp.concatenate([x * m1, x * m2], axis=0)
    dup = lambda x: jnp.concatenate([x, x], axis=0)
    sm = jnp.concatenate([m1, m2], axis=0)
    strict_lower = same_head & ((bd_r % C) > (bd_c % C))
    lower = same_head & ((bd_r % C) >= (bd_c % C))
    eye = bd_r == bd_c
    eye_f = jnp.where(eye, 1.0, 0.0)
    tr = lax.broadcasted_iota(jnp.int32, (C, C), 0)
    tc = lax.broadcasted_iota(jnp.int32, (C, C), 1)
    tri = jnp.where(tr >= tc, 1.0, 0.0).astype(BF16)
    gn_g, gn_b = prow(_P_GNG), prow(_P_GNB)
    sl = lambda ci: slice(ci * C, (ci + 1) * C)

    state = [state_ref[...]]

    def emit_previous(ci):
        both = _dot(q1g_s[ci], state[0])
        ys = both[:2 * C] + q2_s[ci]
        state[0] = both[2 * C:] + h_s[ci]
        mu = jnp.sum(ys, axis=-1, keepdims=True) * (1.0 / HEAD_DIM)
        d = (ys - mu) * sm
        var = jnp.sum(d * d, axis=-1, keepdims=True) * (1.0 / HEAD_DIM)
        yn = d * lax.rsqrt(var + GN_EPS)
        o_ref[0, sl(ci), :] = ((yn[:C] + yn[C:]) * gn_g + gn_b
                               + bonus_s[sl(ci), :]).astype(o_ref.dtype)

    pending = list(chunks)

    def hook():
        if pending:
            emit_previous(pending.pop(0))

    L = [_dot_exact_rhs(tri, lw[sl(ci)]) for ci in chunks]
    Lc = [x[C - 1:C, :] for x in L]
    e_neg = [jnp.exp(-x) for x in L]
    e_end = [jnp.exp(xc - x) for xc, x in zip(Lc, L)]
    Ks = [stack(kap[sl(ci)] * jnp.exp(L[ci] - lw[sl(ci)])).astype(BF16) for ci in chunks]
    Rs = [stack(r[sl(ci)] * jnp.exp(L[ci])) for ci in chunks]
    Vs = [stack(v[sl(ci)]).astype(BF16) for ci in chunks]
    K2 = [dup(k2[sl(ci)] * e_neg[ci]).astype(BF16) for ci in chunks]
    B2 = [dup(b[sl(ci)] * e_neg[ci]).astype(BF16) for ci in chunks]
    KendT = [stack(k2[sl(ci)] * e_end[ci]).T.astype(BF16) for ci in chunks]
    BendT = [stack(b[sl(ci)] * e_end[ci]).T.astype(BF16) for ci in chunks]
    N = [jnp.where(strict_lower, _dot_nt(Ks[ci], B2[ci]), 0.0) for ci in chunks]
    Akk = [jnp.where(strict_lower, _dot_nt(Ks[ci], K2[ci]), 0.0).astype(BF16) for ci in chunks]
    hook()
    Ark = [jnp.where(lower, _dot_nt(Rs[ci], K2[ci]), 0.0).astype(BF16) for ci in chunks]
    Arb = [jnp.where(lower, _dot_nt(Rs[ci], B2[ci]), 0.0).astype(BF16) for ci in chunks]
    hook()
    AkkV = [_dot(Akk[ci], Vs[ci]) for ci in chunks]
    ArkV = [_dot(Ark[ci], Vs[ci]) for ci in chunks]
    KV = [_dot(KendT[ci], Vs[ci]) for ci in chunks]
    hook()
    P = [_dot(x, x) for x in N]
    T = [eye_f - x for x in N]
    for _ in range(C.bit_length() - 3):
        both = [_dot(jnp.concatenate([t, p], axis=0), p) for t, p in zip(T, P)]
        T = [t + x[:2 * C] for t, x in zip(T, both)]
        P = [x[2 * C:] for x in both]
        hook()
    Tb = [(t + _dot(t, p)).astype(BF16) for t, p in zip(T, P)]
    hook()
    P1 = [_dot(Tb[ci], Ks[ci]).astype(BF16) for ci in chunks]
    P2 = [_dot(Tb[ci], AkkV[ci]).astype(BF16) for ci in chunks]
    while pending:
        hook()
    state_ref[...] = state[0]
    for ci in chunks:
        q1g_s[ci, :2 * C, :] = (Rs[ci] - _dot(Arb[ci], P1[ci])).astype(BF16)
        q2_s[ci] = ArkV[ci] - _dot(Arb[ci], P2[ci])
        q1g_s[ci, 2 * C:, :] = (
            jnp.where(eye, jnp.broadcast_to(jnp.exp(Lc[ci]), (LANES, LANES)), 0.0)
            - _dot(BendT[ci], P1[ci])).astype(BF16)
        h_s[ci] = KV[ci] - _dot(BendT[ci], P2[ci])
    bonus_s[...] = bonus


def _rwkv_call(rw, pp, mixwa, wcomb, width, ct):
    bsz, t, _ = rw.shape
    n_pairs = width // LANES
    nt = t // ct
    last = nt - 1
    blk = lambda off: pl.BlockSpec((1, ct, LANES),
                                   lambda i, p, j, off=off: (i, jnp.minimum(j, last), off + p))
    wa_blk = pl.BlockSpec((1, ct, LANES), lambda i, p, j: (i, jnp.minimum(j, last), 3 * n_pairs))
    staged = pltpu.VMEM((ct // CHUNK, LANES, LANES), F32)
    return pl.pallas_call(
        _rwkv_kernel,
        name="rwkv",
        out_shape=jax.ShapeDtypeStruct((bsz, t, width), BF16),
        grid=(bsz, n_pairs, nt + 1),
        in_specs=[blk(0), blk(n_pairs), blk(2 * n_pairs), wa_blk,
                  pl.BlockSpec((1, _P_ROWS, LANES), lambda i, p, j: (p, 0, 0)),
                  pl.BlockSpec((1, LANES), lambda i, p, j: (0, 0)),
                  pl.BlockSpec((1, LANES, 2 * LANES), lambda i, p, j: (p, 0, 0))],
        out_specs=pl.BlockSpec((1, ct, LANES), lambda i, p, j: (i, jnp.maximum(j - 1, 0), p)),
        scratch_shapes=[pltpu.VMEM((SUBLANES, LANES), F32), pltpu.VMEM((LANES, LANES), F32),
                        pltpu.VMEM((ct // CHUNK, 2 * LANES, LANES), BF16), staged, staged,
                        pltpu.VMEM((ct, LANES), F32)],
        compiler_params=_params(("arbitrary", "arbitrary", "arbitrary")),
    )(rw, rw, rw, rw, pp, mixwa, wcomb)


def _fox_kernel(q_ref, qb_ref, kt_ref, kbt_ref, v_ref, o_ref,
                s_buf, p_buf, m_buf, a_buf, acc_buf, tab_ref):
    bq, bk = q_ref.shape[1], kt_ref.shape[3]
    nt = kt_ref.shape[1]
    qi = pl.program_id(2)
    n_full = qi * FOX_Q_BLOCKS
    heads = range(PAIR)
    lane_q = lax.broadcasted_iota(jnp.int32, (bq, LANES), 1)
    lane_v = lax.broadcasted_iota(jnp.int32, (bk, LANES), 1)
    lane_t = lax.broadcasted_iota(jnp.int32, (1, LANES), 1)
    own_q = (lane_q < HEAD_DIM, lane_q >= HEAD_DIM)
    own_v = (lane_v < HEAD_DIM, lane_v >= HEAD_DIM)
    den_lane = (HEAD_DIM, 0)
    table_at = (HEAD_DIM, 0)
    q, qb = q_ref[0], qb_ref[0]
    q_aug = [jnp.where(own_q[h], q, qb) for h in heads]
    v_ones = [jnp.where(lane_v == den_lane[h], 1.0, 0.0).astype(BF16) for h in heads]
    rr = lax.broadcasted_iota(jnp.int32, (bq, bk), 0)
    cc = lax.broadcasted_iota(jnp.int32, (bq, bk), 1)

    @pl.when(qi == 0)
    def _():
        def one(j, tabs):
            kt = kt_ref[0, j].astype(F32)
            kbt = kbt_ref[0, j].astype(F32)
            sq = kt * kt
            new = []
            for h in heads:
                ssq = jnp.sum(sq[h * HEAD_DIM:(h + 1) * HEAD_DIM], axis=0, keepdims=True)
                new.append(jnp.sqrt(jnp.max(ssq, axis=1, keepdims=True)))
            for h in heads:
                t0 = table_at[h]
                ck = -(kbt[t0:t0 + 1] + kbt[t0 + 1:t0 + 2] + kbt[t0 + 2:t0 + 3])
                new.append(jnp.min(ck, axis=1, keepdims=True))
            return tuple(jnp.where(lane_t == j, x, t) for x, t in zip(new, tabs))

        zero = jnp.zeros((1, LANES), F32)
        tabs = lax.fori_loop(0, nt, one, (zero,) * (2 * PAIR))
        for i, t in enumerate(tabs):
            tab_ref[i:i + 1, :] = t

    bd_r = lax.broadcasted_iota(jnp.int32, (LANES, LANES), 0)
    bd_c = lax.broadcasted_iota(jnp.int32, (LANES, LANES), 1)
    ones_bd = jnp.where((bd_r // HEAD_DIM) == (bd_c // HEAD_DIM), 1.0, 0.0).astype(BF16)
    qf, qbf = q.astype(F32), qb.astype(F32)
    qn_lanes = jnp.sqrt(jnp.max(_dot_exact_lhs(qf * qf, ones_bd), axis=0, keepdims=True))
    qn = [qn_lanes[:, h * HEAD_DIM:h * HEAD_DIM + 1] for h in heads]
    cq_max = []
    for h in heads:
        t0 = table_at[h] + N_SPLIT
        cq = qbf[:, t0:t0 + 1] + qbf[:, t0 + 1:t0 + 2] + qbf[:, t0 + 2:t0 + 3]
        cq_max.append(jnp.max(cq, axis=0, keepdims=True))

    wide = lambda x: jnp.concatenate([x] * (bk // LANES), axis=1)

    def scores(j, slot, m_prev, diag=None):
        kt, kbt = kt_ref[0, j], kbt_ref[0, j]
        k_aug = (jnp.concatenate([kt[:HEAD_DIM], kbt[HEAD_DIM:]], axis=0),
                 jnp.concatenate([kbt[:HEAD_DIM], kt[HEAD_DIM:]], axis=0))
        for h in heads:
            s = jnp.dot(q_aug[h], k_aug[h], preferred_element_type=F32)
            if diag is not None:
                s = jnp.where(cc + diag * bk <= rr, s, -jnp.inf)
            s_buf[slot, h] = s
            bmax = jnp.broadcast_to(jnp.max(s, axis=1, keepdims=True), (bq, LANES))
            m_buf[slot, h] = jnp.maximum(m_prev[h], bmax)

    def softmax(slot, m_prev):
        for h in heads:
            m_cur = m_buf[slot, h]
            a_buf[slot, h] = jnp.exp2(m_prev[h] - m_cur)
            p_buf[slot, h] = jnp.exp2(s_buf[slot, h] - wide(m_cur)).astype(BF16)

    def values(j, slot):
        vb = v_ref[0, pl.ds(pl.multiple_of(j * bk, bk), bk), :]
        for h in heads:
            v_aug = jnp.where(own_v[h], vb, v_ones[h])
            acc_buf[h] = a_buf[slot, h] * acc_buf[h] + jnp.dot(p_buf[slot, h], v_aug,
                                                               preferred_element_type=F32)

    def step(j, slot):
        m_before = [m_buf[1 - slot, h] for h in heads]
        scores(j + 1, 1 - slot, [m_buf[slot, h] for h in heads])
        values(jnp.maximum(j - 1, 0), 1 - slot)
        softmax(slot, m_before)

    def drain(last, slot):
        values(jnp.maximum(last - 1, 0), 1 - slot)
        softmax(slot, [m_buf[1 - slot, h] for h in heads])
        values(last, slot)

    never = [jnp.full((bq, LANES), -jnp.inf, F32)] * PAIR
    acc_buf[...] = jnp.zeros_like(acc_buf)
    scores(n_full, 0, never, diag=0)
    softmax(0, never)
    lo = slice(bk, bq)
    kt2, kbt2 = kt_ref[0, n_full + 1], kbt_ref[0, n_full + 1]
    k_aug2 = (jnp.concatenate([kt2[:HEAD_DIM], kbt2[HEAD_DIM:]], axis=0),
              jnp.concatenate([kbt2[:HEAD_DIM], kt2[HEAD_DIM:]], axis=0))
    tri = (lax.broadcasted_iota(jnp.int32, (bk, bk), 1)
           <= lax.broadcasted_iota(jnp.int32, (bk, bk), 0))
    for h in heads:
        s = jnp.dot(q_aug[h][lo], k_aug2[h], preferred_element_type=F32)
        s_buf[1, h, lo, :] = jnp.where(tri, s, -jnp.inf)
    values(n_full, 0)
    vb2 = v_ref[0, pl.ds(pl.multiple_of((n_full + 1) * bk, bk), bk), :]
    for h in heads:
        m_old = m_buf[0, h, lo, :]
        s = s_buf[1, h, lo, :]
        bmax = jnp.broadcast_to(jnp.max(s, axis=1, keepdims=True), (bk, LANES))
        m_lo = jnp.maximum(m_old, bmax)
        p = jnp.exp2(s - wide(m_lo)).astype(BF16)
        v_aug = jnp.where(own_v[h], vb2, v_ones[h])
        acc_buf[h, lo, :] = (jnp.exp2(m_old - m_lo) * acc_buf[h, lo, :]
                             + jnp.dot(p, v_aug, preferred_element_type=F32))
        m_buf[1, h, :bk, :] = m_buf[0, h, :bk, :]
        m_buf[1, h, lo, :] = m_lo

    needed = lane_t < 0
    for h in heads:
        m_min = jnp.min(m_buf[1, h], axis=0, keepdims=True)
        bound = (qn[h] * tab_ref[h:h + 1, :] * (1.0 + 2.0 ** -8) + 1.0
                 + cq_max[h] - tab_ref[PAIR + h:PAIR + h + 1, :])
        needed = jnp.logical_or(needed, bound > m_min - FOX_MARGIN)
    needed = jnp.logical_and(needed, lane_t < n_full)
    first = jnp.min(jnp.where(needed, lane_t, n_full))
    count = n_full - first

    p_buf[1] = jnp.zeros(p_buf.shape[1:], BF16)
    a_buf[1] = jnp.ones(a_buf.shape[1:], F32)

    @pl.when(count > 0)
    def _():
        scores(first, 0, [m_buf[1, h] for h in heads])

    def pair(jj, carry):
        step(first + 2 * jj, 0)
        step(first + 2 * jj + 1, 1)
        return carry

    lax.fori_loop(0, jnp.maximum(count - 1, 0) // 2, pair, 0)
    even = count % 2 == 0

    @pl.when(jnp.logical_and(count >= 2, even))
    def _():
        step(n_full - 2, 0)
        drain(n_full - 1, 1)

    @pl.when(jnp.logical_not(even))
    def _():
        drain(n_full - 1, 0)

    out = [acc_buf[h] / acc_buf[h][:, den_lane[h]:den_lane[h] + 1] for h in heads]
    o_ref[0] = jnp.where(own_q[0], out[0], out[1]).astype(o_ref.dtype)


def _fox_call(q, qb, kt, kbt, v):
    bsz, t, width = q.shape
    nt, bk = kt.shape[1], kt.shape[3]
    bq = FOX_Q_BLOCKS * bk
    n_pairs = width // LANES
    q_blk = pl.BlockSpec((1, bq, LANES), lambda i, p, j: (i, j, p))
    once = pl.Buffered(1)
    kt_blk = pl.BlockSpec((1, nt, LANES, bk), lambda i, p, j: (i, 0, p, 0), pipeline_mode=once)
    v_blk = pl.BlockSpec((1, t, LANES), lambda i, p, j: (i, 0, p), pipeline_mode=once)
    return pl.pallas_call(
        _fox_kernel,
        name="fox",
        out_shape=jax.ShapeDtypeStruct((bsz, t, width), BF16),
        grid=(bsz, n_pairs, t // bq),
        in_specs=[q_blk, q_blk, kt_blk, kt_blk, v_blk],
        out_specs=q_blk,
        scratch_shapes=[pltpu.VMEM((2, PAIR, bq, bk), F32), pltpu.VMEM((2, PAIR, bq, bk), BF16),
                        pltpu.VMEM((2, PAIR, bq, LANES), F32), pltpu.VMEM((2, PAIR, bq, LANES), F32),
                        pltpu.VMEM((PAIR, bq, LANES), F32), pltpu.VMEM((SUBLANES, LANES), F32)],
        compiler_params=_params(("arbitrary", "arbitrary", "arbitrary")),
    )(q, qb, kt, kbt, v)


def _merge_kernel(ya_ref, yb_ref, g_ref, x_ref, gc_ref, wa_ref, wb_ref, lg_ref, lb_ref, o_ref,
                  *, alpha):
    half = ya_ref.shape[2]
    g = g_ref[0].astype(F32)
    silu = g / (1.0 + jnp.exp(-g))
    out = (_dot(ya_ref[0].astype(F32) * silu[:, :half], wa_ref[...])
           + _dot(yb_ref[0].astype(F32) * silu[:, half:], wb_ref[...]))
    z = alpha * x_ref[0] + (1.0 + gc_ref[0]) * out
    o_ref[0] = _layer_norm_rows(z, lg_ref[...], lb_ref[...])


def _merge_call(ya, yb, g, x, gate_c, w_a, w_b, ln_g, ln_b, alpha, tm):
    bsz, t, d = x.shape
    half = ya.shape[2]
    rows = lambda n: pl.BlockSpec((1, tm, n), lambda i, j: (i, j, 0))
    full = lambda a: pl.BlockSpec(a.shape, lambda i, j: (0,) * a.ndim)
    return pl.pallas_call(
        functools.partial(_merge_kernel, alpha=alpha),
        name="merge",
        out_shape=jax.ShapeDtypeStruct(x.shape, F32),
        grid=(bsz, t // tm),
        in_specs=[rows(half), rows(half), rows(g.shape[2]), rows(d),
                  pl.BlockSpec((1, 1, d), lambda i, j: (i, 0, 0)),
                  full(w_a), full(w_b), full(ln_g), full(ln_b)],
        out_specs=rows(d),
        compiler_params=_params(("arbitrary", "arbitrary")),
    )(ya, yb, g, x, gate_c, w_a, w_b, ln_g, ln_b)


def _layer_tables(rwkv_mix, w0, w_up, a0, a_up, k_k, k_a, r_k, gn_g, gn_b, width):
    n_pairs = width // LANES
    per_pair = lambda vct: vct.reshape(n_pairs, 1, LANES)
    rows = [rwkv_mix[0:width], rwkv_mix[width:2 * width], rwkv_mix[2 * width:3 * width],
            w0, a0, k_k, k_a, r_k, gn_g, gn_b]
    pp = jnp.concatenate([per_pair(x) for x in rows]
                         + [jnp.zeros((n_pairs, _P_ROWS - len(rows), LANES), F32)], axis=1)
    mixwa = rwkv_mix[3 * width:].reshape(1, 2 * LORA)
    zeros = jnp.zeros((n_pairs, LORA, LANES), F32)
    wu = w_up.reshape(LORA, n_pairs, LANES).transpose(1, 0, 2)
    au = a_up.reshape(LORA, n_pairs, LANES).transpose(1, 0, 2)
    wcomb = jnp.concatenate([jnp.concatenate([wu, zeros], axis=2),
                             jnp.concatenate([zeros, au], axis=2)], axis=1).astype(BF16)
    return pp, mixwa, wcomb


def _fox_tables(heads, tm):
    width = heads * HEAD_DIM
    chan = jnp.arange(width)
    off = chan % LANES
    src = jnp.arange(LANES)[:, None]
    base = lambda hd: (hd // PAIR) * LANES + jnp.where(hd % PAIR == 0, HEAD_DIM, 0)
    at = lambda slot: (chan[None, :] == base(src) + slot) & (src < heads)
    selq = jnp.stack([at(N_SPLIT + i) for i in range(N_SPLIT)]).astype(BF16)
    selk = jnp.stack([at(i).T for i in range(N_SPLIT)]).astype(BF16)
    slot_of = off % HEAD_DIM
    onesq = (slot_of < N_SPLIT).astype(F32).reshape(1, width)
    onesk = jnp.broadcast_to(((slot_of >= N_SPLIT) & (slot_of < 2 * N_SPLIT))
                             .astype(F32)[:, None], (width, tm))
    return selq, onesq, selk, onesk


def kernel(x, c, emb_ln_g, emb_ln_b, w_ada, b_ada, w_in, rwkv_mix, w0, w_up, a0, a_up, k_k, k_a,
           r_k, gn_g, gn_b, fox_bf, w_out, ln_g, ln_b):
    bsz, t, d = x.shape
    depth = w_ada.shape[0]
    width = d // 2
    heads = width // HEAD_DIM
    rw_end = 3 * width + 2 * LORA
    fx_end = rw_end + 3 * width + heads
    alpha = (2 * depth) ** 0.25
    tm = min(TILE, t)

    c8 = jnp.pad(c, ((0, SUBLANES - bsz), (0, 0)))
    mods = _ada_call(c8, w_ada, b_ada)[:, :bsz]
    selq, onesq, selk, onesk = _fox_tables(heads, tm)

    h = x
    emb_g, emb_b = emb_ln_g.reshape(1, d), emb_ln_b.reshape(1, d)
    for l in range(depth):
        shift = mods[l, :, 0:d].reshape(bsz, 1, d)
        scale = mods[l, :, d:2 * d].reshape(bsz, 1, d)
        gate_c = mods[l, :, 2 * d:].reshape(bsz, 1, d)
        w = w_in[l]
        w_rw = w[:, :rw_end].astype(BF16)
        w_q = (w[:, rw_end:rw_end + width] * (HEAD_DIM ** -0.5 * LOG2E)).astype(BF16)
        w_kt = w[:, rw_end + width:rw_end + 2 * width].T.astype(BF16)
        w_v = w[:, rw_end + 2 * width:rw_end + 3 * width].astype(BF16)
        w_f = jnp.pad(w[:, rw_end + 3 * width:fx_end], ((0, 0), (0, LANES - heads))).astype(BF16)
        w_g = w[:, fx_end:].astype(BF16)
        bf = jnp.pad(fox_bf[l], (0, LANES - heads)).reshape(1, LANES)
        outs = _proj_call(h, scale, shift, emb_g, emb_b, w_rw, w_q, w_kt, w_v, w_g, w_f, bf,
                          selq, onesq, selk, onesk, tm, normalize=(l == 0))
        rw, q, kt, v, g, qb, kbt = outs[:7]
        if l == 0:
            h = outs[7]

        pp, mixwa, wcomb = _layer_tables(rwkv_mix[l], w0[l], w_up[l], a0[l], a_up[l], k_k[l],
                                         k_a[l], r_k[l], gn_g[l], gn_b[l], width)
        ya = _rwkv_call(rw, pp, mixwa, wcomb, width, tm)
        yb = _fox_call(q, qb, kt, kbt, v)
        wo = w_out[l].astype(BF16)
        h = _merge_call(ya, yb, g, h, gate_c, wo[:width], wo[width:], ln_g[l].reshape(1, d),
                        ln_b[l].reshape(1, d), alpha, tm)
    return h
```
